```python
import jax, jax.numpy as jnp
from jax import lax
import numpy as np

D_MODEL = 1024
BATCH = 16
SEQ = 4096
DEPTH = 4
DEC_BATCH = 4
DEC_SEQ = 4096
PAST_LEN = 128

N_MIXERS = 4
GROUP_WIDTH = D_MODEL // N_MIXERS
MIX_WIDTH = N_MIXERS * GROUP_WIDTH
HEAD_DIM = 64
CONV_WIDTH = 31
CONV_GROUPS = GROUP_WIDTH // HEAD_DIM
SWA_HEADS = GROUP_WIDTH // HEAD_DIM
SWA_KV_HEADS = 2
SWA_REP = SWA_HEADS // SWA_KV_HEADS
WINDOW = 128
BLOCK = 128
ROPE_THETA = 500000.0
ROPE_DIM = HEAD_DIM // 4
FNET_GROUPS = 4
FNET_CH = GROUP_WIDTH // FNET_GROUPS
NAT_HEADS = GROUP_WIDTH // HEAD_DIM
GRID_W = 64
NAT_KH = 8
NAT_KW = 16
N_EXPERT_GROUPS = 4
EXPERTS_PER_GROUP = 8
N_EXPERTS = N_EXPERT_GROUPS * EXPERTS_PER_GROUP
TOP_K = 2
D_EXPERT = D_MODEL // 2
MOE_BLOCK = 256
EPS = 1e-6

OFF_A = 0
OFF_BQ = OFF_A + 2 * GROUP_WIDTH
OFF_BK = OFF_BQ + SWA_HEADS * HEAD_DIM
OFF_BV = OFF_BK + SWA_KV_HEADS * HEAD_DIM
OFF_C = OFF_BV + SWA_KV_HEADS * HEAD_DIM
OFF_DQ = OFF_C + GROUP_WIDTH
OFF_DK = OFF_DQ + NAT_HEADS * HEAD_DIM
OFF_DV = OFF_DK + NAT_HEADS * HEAD_DIM
IN_WIDTH = OFF_DV + NAT_HEADS * HEAD_DIM

kernel_name = 'hybrid_conv_swa_fnet_natten_hmoe_encoder'


def rmsnorm(x, g):
    xf = x.astype(jnp.float32)
    y = xf * lax.rsqrt(jnp.mean(xf * xf, -1, keepdims=True) + EPS)
    return (y * g.astype(jnp.float32)).astype(x.dtype)


def group_rmsnorm(y, g):
    B, S, _ = y.shape
    yf = y.astype(jnp.float32).reshape(B, S, N_MIXERS, GROUP_WIDTH)
    yf = yf * lax.rsqrt(jnp.mean(yf * yf, -1, keepdims=True) + EPS)
    return (yf.reshape(B, S, MIX_WIDTH) * g.astype(jnp.float32)).astype(y.dtype)


def partial_rope(x, pos):
    half = ROPE_DIM // 2
    inv = jnp.float32(ROPE_THETA) ** (-jnp.arange(half, dtype=jnp.float32) * 2.0 / ROPE_DIM)
    ang = pos.astype(jnp.float32)[:, None] * inv[None, :]
    cos = jnp.cos(ang)[None, :, None, :]
    sin = jnp.sin(ang)[None, :, None, :]
    xf = x.astype(jnp.float32)
    x1 = xf[..., :half]
    x2 = xf[..., half:ROPE_DIM]
    out = jnp.concatenate([x1 * cos - x2 * sin, x2 * cos + x1 * sin, xf[..., ROPE_DIM:]], -1)
    return out.astype(x.dtype)


def conformer_conv(u, conv_w, conv_b, norm_g, norm_b):
    B, S, _ = u.shape
    a, g = jnp.split(u, 2, axis=-1)
    v = a * jax.nn.sigmoid(g)
    filt = conv_w[:, None, :].astype(v.dtype)
    v = lax.conv_general_dilated(v, filt, window_strides=(1,),
                                 padding=[(CONV_WIDTH // 2, CONV_WIDTH // 2)],
                                 dimension_numbers=('NWC', 'WIO', 'NWC'),
                                 feature_group_count=GROUP_WIDTH) + conv_b.astype(v.dtype)
    vf = v.astype(jnp.float32).reshape(B, S, CONV_GROUPS, GROUP_WIDTH // CONV_GROUPS)
    mu = jnp.mean(vf, -1, keepdims=True)
    var = jnp.mean(jnp.square(vf - mu), -1, keepdims=True)
    vf = ((vf - mu) * lax.rsqrt(var + EPS)).reshape(B, S, GROUP_WIDTH)
    vf = vf * norm_g.astype(jnp.float32) + norm_b.astype(jnp.float32)
    return jax.nn.silu(vf).astype(u.dtype)


def window_attention(q, k, v, sink):
    B, S = q.shape[0], q.shape[1]
    nb = S // BLOCK
    qb = q.reshape(B, nb, BLOCK, SWA_KV_HEADS, SWA_REP, HEAD_DIM)

    def band(t):
        tp = jnp.pad(t, ((0, 0), (BLOCK, BLOCK), (0, 0), (0, 0)))
        tp = tp.reshape(B, nb + 2, BLOCK, SWA_KV_HEADS, HEAD_DIM)
        return jnp.concatenate([tp[:, j:j + nb] for j in range(3)], axis=2)

    kb, vb = band(k), band(v)
    s = jnp.einsum('bnqgrd,bnkgd->bngrqk', qb, kb).astype(jnp.float32) * (HEAD_DIM ** -0.5)
    blk = jnp.arange(nb)[:, None] * BLOCK
    qpos = blk + jnp.arange(BLOCK)[None, :]
    kpos = blk - BLOCK + jnp.arange(3 * BLOCK)[None, :]
    valid = (jnp.abs(qpos[:, :, None] - kpos[:, None, :]) <= WINDOW) & (kpos[:, None, :] >= 0) & (kpos[:, None, :] < S)
    s = jnp.where(valid[None, :, None, None], s, -jnp.inf)
    sink_f = sink.astype(jnp.float32).reshape(SWA_KV_HEADS, SWA_REP)[None, None, :, :, None, None]
    m = jnp.maximum(jnp.max(s, -1, keepdims=True), sink_f)
    p = jnp.exp(s - m)
    p = p / (jnp.sum(p, -1, keepdims=True) + jnp.exp(sink_f - m))
    o = jnp.einsum('bngrqk,bnkgd->bnqgrd', p.astype(v.dtype), vb)
    return o.reshape(B, S, SWA_HEADS * HEAD_DIM)


def fourier_mix(u):
    B, S, _ = u.shape
    uf = u.astype(jnp.float32).reshape(B, S, FNET_GROUPS, FNET_CH)
    y = jnp.fft.fft2(uf, axes=(1, 3), norm='ortho').real
    return y.reshape(B, S, GROUP_WIDTH).astype(u.dtype)


def neighbourhood_attention(q, k, v, rpb):
    B, S = q.shape[0], q.shape[1]
    rows = S // GRID_W
    kh = min(NAT_KH, rows)
    r = jnp.arange(rows)
    key_rows = jnp.clip(r - kh // 2, 0, rows - kh)[:, None] + jnp.arange(kh)[None, :]
    c = jnp.arange(GRID_W)
    c0 = jnp.clip(c - NAT_KW // 2, 0, GRID_W - NAT_KW)
    col_ok = (c[None, :] >= c0[:, None]) & (c[None, :] < c0[:, None] + NAT_KW)
    qg = q.reshape(B, rows, GRID_W, NAT_HEADS, HEAD_DIM)
    kg = k.reshape(B, rows, GRID_W, NAT_HEADS, HEAD_DIM)[:, key_rows]
    vg = v.reshape(B, rows, GRID_W, NAT_HEADS, HEAD_DIM)[:, key_rows]
    s = jnp.einsum('brqhd,brikhd->brhqik', qg, kg).astype(jnp.float32) * (HEAD_DIM ** -0.5)
    dr = key_rows - r[:, None] + (NAT_KH - 1)
    dc = jnp.clip(c[None, :] - c[:, None] + (NAT_KW - 1), 0, 2 * NAT_KW - 2)
    bias = rpb.astype(jnp.float32)[:, dr][..., dc]
    s = s + jnp.transpose(bias, (1, 0, 3, 2, 4))[None]
    s = jnp.where(col_ok[:, None, :], s, -jnp.inf)
    p = jax.nn.softmax(s.reshape(B, rows, NAT_HEADS, GRID_W, kh * GRID_W), axis=-1)
    p = p.reshape(B, rows, NAT_HEADS, GRID_W, kh, GRID_W).astype(v.dtype)
    o = jnp.einsum('brhqik,brikhd->brqhd', p, vg)
    return o.reshape(B, S, NAT_HEADS * HEAD_DIM)


def grouped_experts(xf, expert, gate, w_gate, w_up, w_down):
    N, D = xf.shape
    A = expert.shape[0]
    tok = jnp.arange(A, dtype=jnp.int32) // TOP_K
    order = jnp.argsort(expert)
    e_sorted = expert[order]
    counts = jnp.bincount(expert, length=N_EXPERTS)
    padded = (counts + MOE_BLOCK - 1) // MOE_BLOCK * MOE_BLOCK
    start = jnp.cumsum(counts) - counts
    pend = jnp.cumsum(padded)
    pstart = pend - padded
    dest = pstart[e_sorted] + jnp.arange(A, dtype=jnp.int32) - start[e_sorted]
    n_blocks = -(-A // MOE_BLOCK) + N_EXPERTS
    P = n_blocks * MOE_BLOCK
    slot_tok = jnp.full((P,), N, jnp.int32).at[dest].set(tok[order])
    slot_gate = jnp.zeros((P,), gate.dtype).at[dest].set(gate[order])
    block_expert = jnp.clip(jnp.searchsorted(pend, jnp.arange(n_blocks) * MOE_BLOCK, side='right'), 0, N_EXPERTS - 1)
    x_pad = jnp.concatenate([xf, jnp.zeros((1, D), xf.dtype)], 0)

    def run_block(args):
        idx, e = args
        xb = x_pad[idx]
        hid = jax.nn.silu(xb @ w_gate[e]) * (xb @ w_up[e])
        return hid @ w_down[e]

    out = lax.map(run_block, (slot_tok.reshape(n_blocks, MOE_BLOCK), block_expert))
    out = out.reshape(P, D) * slot_gate[:, None].astype(out.dtype)
    y = jnp.zeros_like(x_pad).at[slot_tok].add(out)
    return y[:N]


def hierarchical_moe(h, wr_g, br_g, wr_e, br_e, w_gate, w_up, w_down):
    B, S, D = h.shape
    N = B * S
    hf = h.reshape(N, D)
    g_logits = (hf @ wr_g).astype(jnp.float32) + br_g.astype(jnp.float32)
    g_prob = jax.nn.softmax(g_logits, -1)
    grp = jnp.argmax(g_logits, -1).astype(jnp.int32)
    p_grp = jnp.take_along_axis(g_prob, grp[:, None], -1)
    e_logits = ((hf @ wr_e).astype(jnp.float32) + br_e.astype(jnp.float32)).reshape(N, N_EXPERT_GROUPS, EXPERTS_PER_GROUP)
    e_logits = jnp.take_along_axis(e_logits, grp[:, None, None], 1)[:, 0]
    top_l, top_i = lax.top_k(e_logits, TOP_K)
    gate = (jax.nn.softmax(top_l, -1) * p_grp).astype(h.dtype)
    expert = (grp[:, None] * EXPERTS_PER_GROUP + top_i).astype(jnp.int32)
    y = grouped_experts(hf, expert.reshape(-1), gate.reshape(-1), w_gate, w_up, w_down)
    return y.reshape(B, S, D)


def encoder_trunk(x, norm1_g, w_in, conv_w, conv_b, conv_norm_g, conv_norm_b, attn_sink, nat_rpb,
                  mix_norm_g, w_out, norm2_g, router_group_w, router_group_b, router_expert_w,
                  router_expert_b, expert_w_gate, expert_w_up, expert_w_down, final_norm_g):
    B, S, _ = x.shape
    pos = jnp.arange(S)
    for l in range(DEPTH):
        h = rmsnorm(x, norm1_g[l])
        u = h @ w_in[l]
        ya = conformer_conv(u[..., OFF_A:OFF_BQ], conv_w[l], conv_b[l], conv_norm_g[l], conv_norm_b[l])
        qb = partial_rope(u[..., OFF_BQ:OFF_BK].reshape(B, S, SWA_HEADS, HEAD_DIM), pos)
        kb = partial_rope(u[..., OFF_BK:OFF_BV].reshape(B, S, SWA_KV_HEADS, HEAD_DIM), pos)
        vb = u[..., OFF_BV:OFF_C].reshape(B, S, SWA_KV_HEADS, HEAD_DIM)
        yb = window_attention(qb, kb, vb, attn_sink[l])
        yc = fourier_mix(u[..., OFF_C:OFF_DQ])
        qd = u[..., OFF_DQ:OFF_DK].reshape(B, S, NAT_HEADS, HEAD_DIM)
        kd = u[..., OFF_DK:OFF_DV].reshape(B, S, NAT_HEADS, HEAD_DIM)
        vd = u[..., OFF_DV:IN_WIDTH].reshape(B, S, NAT_HEADS, HEAD_DIM)
        yd = neighbourhood_attention(qd, kd, vd, nat_rpb[l])
        y = group_rmsnorm(jnp.concatenate([ya, yb, yc, yd], -1), mix_norm_g[l])
        x = x + y @ w_out[l]
        x = x + hierarchical_moe(rmsnorm(x, norm2_g[l]), router_group_w[l], router_group_b[l],
                                 router_expert_w[l], router_expert_b[l], expert_w_gate[l],
                                 expert_w_up[l], expert_w_down[l])
    return rmsnorm(x, final_norm_g)


def setup_inputs(seed: int = 0) -> dict:
    key = jax.random.key(seed)
    ks = jax.random.split(key, 24)
    f32 = jnp.float32
    L = DEPTH

    def nrm(k, shape, scale):
        return jax.random.normal(k, shape, f32) * scale

    return {
        'x_prompt': nrm(ks[0], (BATCH, SEQ, D_MODEL), 1.0),
        'x_sample': nrm(ks[1], (DEC_BATCH, DEC_SEQ, D_MODEL), 1.0),
        'norm1_g': 1.0 + nrm(ks[2], (L, D_MODEL), 0.01),
        'w_in': nrm(ks[3], (L, D_MODEL, IN_WIDTH), D_MODEL ** -0.5),
        'conv_w': nrm(ks[4], (L, CONV_WIDTH, GROUP_WIDTH), CONV_WIDTH ** -0.5),
        'conv_b': nrm(ks[5], (L, GROUP_WIDTH), 0.01),
        'conv_norm_g': 1.0 + nrm(ks[6], (L, GROUP_WIDTH), 0.01),
        'conv_norm_b': nrm(ks[7], (L, GROUP_WIDTH), 0.01),
        'attn_sink': nrm(ks[8], (L, SWA_HEADS), 0.5),
        'nat_rpb': nrm(ks[9], (L, NAT_HEADS, 2 * NAT_KH - 1, 2 * NAT_KW - 1), 0.02),
        'mix_norm_g': 1.0 + nrm(ks[10], (L, MIX_WIDTH), 0.01),
        'w_out': nrm(ks[11], (L, MIX_WIDTH, D_MODEL), MIX_WIDTH ** -0.5),
        'norm2_g': 1.0 + nrm(ks[12], (L, D_MODEL), 0.01),
        'router_group_w': nrm(ks[13], (L, D_MODEL, N_EXPERT_GROUPS), D_MODEL ** -0.5),
        'router_group_b': nrm(ks[14], (L, N_EXPERT_GROUPS), 0.01),
        'router_expert_w': nrm(ks[15], (L, D_MODEL, N_EXPERTS), D_MODEL ** -0.5),
        'router_expert_b': nrm(ks[16], (L, N_EXPERTS), 0.01),
        'expert_w_gate': nrm(ks[17], (L, N_EXPERTS, D_MODEL, D_EXPERT), D_MODEL ** -0.5),
        'expert_w_up': nrm(ks[18], (L, N_EXPERTS, D_MODEL, D_EXPERT), D_MODEL ** -0.5),
        'expert_w_down': nrm(ks[19], (L, N_EXPERTS, D_EXPERT, D_MODEL), D_EXPERT ** -0.5),
        'final_norm_g': 1.0 + nrm(ks[20], (D_MODEL,), 0.01),
    }


def reference(x_prompt, x_sample, norm1_g, w_in, conv_w, conv_b, conv_norm_g, conv_norm_b, attn_sink,
              nat_rpb, mix_norm_g, w_out, norm2_g, router_group_w, router_group_b, router_expert_w,
              router_expert_b, expert_w_gate, expert_w_up, expert_w_down, final_norm_g):
    y_prompt = encoder_trunk(x_prompt, norm1_g, w_in, conv_w, conv_b, conv_norm_g, conv_norm_b, attn_sink,
                             nat_rpb, mix_norm_g, w_out, norm2_g, router_group_w, router_group_b,
                             router_expert_w, router_expert_b, expert_w_gate, expert_w_up, expert_w_down,
                             final_norm_g)
    y_sample = encoder_trunk(x_sample, norm1_g, w_in, conv_w, conv_b, conv_norm_g, conv_norm_b, attn_sink,
                             nat_rpb, mix_norm_g, w_out, norm2_g, router_group_w, router_group_b,
                             router_expert_w, router_expert_b, expert_w_gate, expert_w_up, expert_w_down,
                             final_norm_g)
    return (y_prompt, y_sample)
```

```python
import functools
import math

import numpy as np
import jax
import jax.numpy as jnp
from jax import lax
from jax.experimental import pallas as pl
from jax.experimental.pallas import tpu as pltpu

F32 = jnp.float32
BF16 = jnp.bfloat16

D_MODEL = 1024
GROUP_WIDTH = 256
HEAD_DIM = 64
CONV_WIDTH = 31
CONV_HALO = 16
SWA_HEADS = 4
SWA_KV_HEADS = 2
WINDOW = 128
ROPE_THETA = 500000.0
ROPE_DIM = 16
FNET_CH = 64
FNET_S2 = 256
NAT_HEADS = 4
GRID_W = 64
NAT_KH = 8
NAT_KW = 16
N_EXPERT_GROUPS = 4
EXPERTS_PER_GROUP = 8
N_EXPERTS = 32
D_EXPERT = 512
EPS = 1e-6
IN_WIDTH = 2048
NEG = -1e30

COL_CONV = 0
COL_SWA_Q = 512
COL_SWA_KV = 768
COL_FNET = 1024
COL_NAT_Q = 1280
COL_NAT_K = 1536
COL_NAT_V = 1792

TM = 512
MOE_BLOCK = 256
VMEM_LIMIT = 56 * 1024 * 1024


def _cparams(*sem):
    return pltpu.CompilerParams(dimension_semantics=sem, vmem_limit_bytes=VMEM_LIMIT)


def _split_dot(a, m_bf16):
    hi = a.astype(BF16)
    lo = (a - hi.astype(F32)).astype(BF16)
    return (jnp.dot(hi, m_bf16, preferred_element_type=F32)
            + jnp.dot(lo, m_bf16, preferred_element_type=F32))


def _in_proj_kernel(x_ref, g_ref, w_ref, cos_ref, sina_ref, sinb_ref, u_ref):
    x = x_ref[...]
    h = x * lax.rsqrt(jnp.mean(x * x, axis=-1, keepdims=True) + EPS) * g_ref[...]
    u = jnp.dot(h.astype(BF16), w_ref[...], preferred_element_type=F32)
    u_ref[:, :COL_SWA_Q] = u[:, :COL_SWA_Q].astype(BF16)
    cosf, sina, sinb = cos_ref[...], sina_ref[...], sinb_ref[...]
    for c in range(COL_SWA_Q, COL_SWA_KV + 128, 128):
        t = u[:, c:c + 128]
        fwd = pltpu.roll(t, 128 - ROPE_DIM // 2, axis=1)
        bwd = pltpu.roll(t, ROPE_DIM // 2, axis=1)
        u_ref[:, c:c + 128] = (t * cosf + fwd * sina + bwd * sinb).astype(BF16)
    u_ref[:, COL_SWA_KV + 128:] = u[:, COL_SWA_KV + 128:].astype(BF16)


def _in_proj(x, g, w_bf16, rope, seq):
    n = x.shape[0]
    tiles_per_seq = seq // TM
    cosf, sina, sinb = rope
    rope_spec = pl.BlockSpec((TM, 128), lambda i: (i % tiles_per_seq, 0))
    return pl.pallas_call(
        _in_proj_kernel,
        grid=(n // TM,),
        in_specs=[pl.BlockSpec((TM, D_MODEL), lambda i: (i, 0)),
                  pl.BlockSpec((1, D_MODEL), lambda i: (0, 0)),
                  pl.BlockSpec((D_MODEL, IN_WIDTH), lambda i: (0, 0)),
                  rope_spec, rope_spec, rope_spec],
        out_specs=pl.BlockSpec((TM, IN_WIDTH), lambda i: (i, 0)),
        out_shape=jax.ShapeDtypeStruct((n, IN_WIDTH), BF16),
        compiler_params=_cparams("parallel"),
        name="in_proj",
    )(x, g.reshape(1, D_MODEL), w_bf16, cosf, sina, sinb)


def _rope_tables(seq):
    half = ROPE_DIM // 2
    inv = jnp.float32(ROPE_THETA) ** (-jnp.arange(half, dtype=F32) * 2.0 / ROPE_DIM)
    ang = jnp.arange(seq).astype(F32)[:, None] * inv[None, :]
    cos, sin = jnp.cos(ang), jnp.sin(ang)
    ones = jnp.ones((seq, HEAD_DIM - ROPE_DIM), F32)
    zeros8 = jnp.zeros((seq, half), F32)
    zeros48 = jnp.zeros((seq, HEAD_DIM - ROPE_DIM), F32)
    cosf = jnp.concatenate([cos, cos, ones], -1)
    sina = jnp.concatenate([-sin, zeros8, zeros48], -1)
    sinb = jnp.concatenate([zeros8, sin, zeros48], -1)
    return tuple(jnp.tile(t, (1, 128 // HEAD_DIM)) for t in (cosf, sina, sinb))


CONV_CHUNK = 128


def _conv_kernel(u_ref, w_ref, b_ref, ng_ref, nb_ref, avg_ref, o_ref, vpad_ref, *, seq):
    zeros = jnp.zeros((CONV_HALO, GROUP_WIDTH), F32)
    vpad_ref[0:CONV_HALO, :] = zeros
    vpad_ref[CONV_HALO + seq:CONV_HALO + seq + CONV_HALO, :] = zeros

    def glu(i, carry):
        r = pl.multiple_of(i * CONV_CHUNK, CONV_CHUNK)
        blk = u_ref[pl.ds(r, CONV_CHUNK), :].astype(F32)
        a, g = blk[:, :GROUP_WIDTH], blk[:, GROUP_WIDTH:]
        vpad_ref[pl.ds(CONV_HALO + r, CONV_CHUNK), :] = a / (1.0 + jnp.exp(-g))
        return carry

    lax.fori_loop(0, seq // CONV_CHUNK, glu, 0)
    avg = avg_ref[...]
    bias, ng, nb = b_ref[...], ng_ref[...], nb_ref[...]

    def conv(i, carry):
        r = pl.multiple_of(i * CONV_CHUNK, CONV_CHUNK)
        acc = jnp.zeros((CONV_CHUNK, GROUP_WIDTH), F32) + bias
        win_rows = CONV_CHUNK + 2 * CONV_HALO
        win = vpad_ref[pl.ds(r, win_rows), :]
        for b in range(8):
            shifted = win if b == 0 else pltpu.roll(win, win_rows - b, axis=0)
            for a in range(4):
                j = 8 * a + b - (CONV_HALO - CONV_WIDTH // 2)
                if 0 <= j < CONV_WIDTH:
                    acc = acc + w_ref[j:j + 1, :] * shifted[8 * a:8 * a + CONV_CHUNK, :]
        mu = _split_dot(acc, avg)
        d = acc - mu
        var = _split_dot(d * d, avg)
        y = d * lax.rsqrt(var + EPS) * ng + nb
        o_ref[pl.ds(r, CONV_CHUNK), :] = (y / (1.0 + jnp.exp(-y))).astype(BF16)
        return carry

    lax.fori_loop(0, seq // CONV_CHUNK, conv, 0)


def _conv_mixer(u, conv_w, conv_b, norm_g, norm_b, nseq, seq):
    gid = np.arange(GROUP_WIDTH) // HEAD_DIM
    avg = jnp.asarray((gid[:, None] == gid[None, :]).astype(np.float32) / HEAD_DIM, BF16)
    row = lambda v: v.reshape(1, GROUP_WIDTH)
    const = lambda shape: pl.BlockSpec(shape, lambda b: (0, 0))
    return pl.pallas_call(
        functools.partial(_conv_kernel, seq=seq),
        grid=(nseq,),
        in_specs=[pl.BlockSpec((seq, 2 * GROUP_WIDTH), lambda b: (b, COL_CONV // (2 * GROUP_WIDTH))),
                  const((CONV_WIDTH, GROUP_WIDTH)), const((1, GROUP_WIDTH)), const((1, GROUP_WIDTH)),
                  const((1, GROUP_WIDTH)), const((GROUP_WIDTH, GROUP_WIDTH))],
        out_specs=pl.BlockSpec((seq, GROUP_WIDTH), lambda b: (b, 0)),
        out_shape=jax.ShapeDtypeStruct((nseq * seq, GROUP_WIDTH), BF16),
        scratch_shapes=[pltpu.VMEM((seq + 2 * CONV_HALO, GROUP_WIDTH), F32)],
        compiler_params=_cparams("parallel"),
        name="conv_mixer",
    )(u, conv_w, row(conv_b), row(norm_g), row(norm_b), avg)


SWA_TQ = 256


def _swa_kernel(sink_ref, q_ref, kv_ref, o_ref, *, seq):
    i = pl.program_id(1)
    kw = SWA_TQ + 2 * WINDOW
    kstart = pl.multiple_of(jnp.clip(i * SWA_TQ - WINDOW, 0, seq - kw), WINDOW)
    kv = kv_ref[pl.ds(kstart, kw), :]
    q = q_ref[...]
    qpos = i * SWA_TQ + lax.broadcasted_iota(jnp.int32, (SWA_TQ, kw), 0)
    kpos = kstart + lax.broadcasted_iota(jnp.int32, (SWA_TQ, kw), 1)
    valid = jnp.abs(qpos - kpos) <= WINDOW
    rep = SWA_HEADS // SWA_KV_HEADS
    for g in range(SWA_KV_HEADS):
        k = kv[:, g * HEAD_DIM:(g + 1) * HEAD_DIM]
        v = kv[:, SWA_KV_HEADS * HEAD_DIM + g * HEAD_DIM:SWA_KV_HEADS * HEAD_DIM + (g + 1) * HEAD_DIM]
        for r in range(rep):
            h = g * rep + r
            qh = q[:, h * HEAD_DIM:(h + 1) * HEAD_DIM]
            s = lax.dot_general(qh, k, (((1,), (1,)), ((), ())), preferred_element_type=F32)
            s = jnp.where(valid, s * (HEAD_DIM ** -0.5), NEG)
            sink = sink_ref[h]
            m = jnp.maximum(jnp.max(s, axis=-1, keepdims=True), sink)
            p = jnp.exp(s - m)
            denom = jnp.sum(p, axis=-1, keepdims=True) + jnp.exp(sink - m)
            o = jnp.dot(p.astype(BF16), v, preferred_element_type=F32) / denom
            o_ref[:, h * HEAD_DIM:(h + 1) * HEAD_DIM] = o.astype(BF16)


def _swa_mixer(u, sink, nseq, seq):
    nq = seq // SWA_TQ
    return pl.pallas_call(
        functools.partial(_swa_kernel, seq=seq),
        grid=(nseq, nq),
        in_specs=[pl.BlockSpec(memory_space=pltpu.SMEM),
                  pl.BlockSpec((SWA_TQ, GROUP_WIDTH), lambda b, i: (b * nq + i, COL_SWA_Q // GROUP_WIDTH)),
                  pl.BlockSpec((seq, GROUP_WIDTH), lambda b, i: (b, COL_SWA_KV // GROUP_WIDTH))],
        out_specs=pl.BlockSpec((SWA_TQ, GROUP_WIDTH), lambda b, i: (b * nq + i, 0)),
        out_shape=jax.ShapeDtypeStruct((nseq * seq, GROUP_WIDTH), BF16),
        compiler_params=_cparams("parallel", "arbitrary"),
        name="swa_mixer",
    )(sink, u, u)


def _fnet_kernel(x_ref, m_ref, cc_ref, sc_ref, o_ref, z_ref, *, seq):
    r1 = seq // FNET_S2
    half = r1 // 2
    xs = [x_ref[s1 * FNET_S2:(s1 + 1) * FNET_S2, :].astype(F32) for s1 in range(r1)]
    sym = {s1: xs[s1] + xs[r1 - s1] for s1 in range(1, half)}
    asym = {s1: xs[s1] - xs[r1 - s1] for s1 in range(1, half)}

    def weighted(terms):
        acc = None
        for w, t in terms:
            if abs(w) < 1e-9:
                continue
            if abs(w - 1.0) < 1e-9:
                acc = t if acc is None else acc + t
            elif abs(w + 1.0) < 1e-9:
                acc = -t if acc is None else acc - t
            else:
                acc = w * t if acc is None else acc + w * t
        return jnp.zeros((FNET_S2, GROUP_WIDTH), F32) if acc is None else acc

    for k1 in range(half + 1):
        re_terms = [(1.0, xs[0]), (float((-1) ** k1), xs[half])]
        re_terms += [(math.cos(2 * math.pi * k1 * s1 / r1), sym[s1]) for s1 in range(1, half)]
        im_terms = [(-math.sin(2 * math.pi * k1 * s1 / r1), asym[s1]) for s1 in range(1, half)]
        zr = weighted(re_terms).astype(BF16)
        zi = weighted(im_terms)
        z_ref[k1, :FNET_S2, :] = zr
        z_ref[k1, FNET_S2:, :] = zi.astype(BF16)
        if 0 < k1 < half:
            z_ref[r1 - k1, :FNET_S2, :] = zr
            z_ref[r1 - k1, FNET_S2:, :] = (-zi).astype(BF16)

    scale = 1.0 / math.sqrt(seq * FNET_CH)
    cc, sc = cc_ref[...], sc_ref[...]
    for k1 in range(r1):
        y = jnp.dot(m_ref[k1], z_ref[k1], preferred_element_type=F32)
        out = (jnp.dot(y[:FNET_S2].astype(BF16), cc, preferred_element_type=F32)
               + jnp.dot(y[FNET_S2:].astype(BF16), sc, preferred_element_type=F32))
        o_ref[:, k1 * GROUP_WIDTH:(k1 + 1) * GROUP_WIDTH] = (out * scale).astype(BF16)


def _fnet_constants(seq):
    r1 = seq // FNET_S2
    k = (np.arange(r1)[:, None, None] + r1 * np.arange(FNET_S2)[None, :, None]).astype(np.float64)
    s2 = np.arange(FNET_S2)[None, None, :].astype(np.float64)
    ang = 2 * np.pi * ((k * s2) % seq) / seq
    c, s = np.cos(ang), np.sin(ang)
    m = np.concatenate([np.concatenate([c, s], -1), np.concatenate([-s, c], -1)], 1)
    ch = np.arange(GROUP_WIDTH)
    same = (ch[:, None] // FNET_CH) == (ch[None, :] // FNET_CH)
    ang_c = 2 * np.pi * ((ch[:, None] % FNET_CH) * (ch[None, :] % FNET_CH) % FNET_CH) / FNET_CH
    cc = np.where(same, np.cos(ang_c), 0.0)
    sc = np.where(same, np.sin(ang_c), 0.0)
    return (jnp.asarray(m, BF16), jnp.asarray(cc, BF16), jnp.asarray(sc, BF16))


def _fnet_mixer(u, consts, nseq, seq):
    r1 = seq // FNET_S2
    m, cc, sc = consts
    out = pl.pallas_call(
        functools.partial(_fnet_kernel, seq=seq),
        grid=(nseq,),
        in_specs=[pl.BlockSpec((seq, GROUP_WIDTH), lambda b: (b, COL_FNET // GROUP_WIDTH)),
                  pl.BlockSpec((r1, 2 * FNET_S2, 2 * FNET_S2), lambda b: (0, 0, 0)),
                  pl.BlockSpec((GROUP_WIDTH, GROUP_WIDTH), lambda b: (0, 0)),
                  pl.BlockSpec((GROUP_WIDTH, GROUP_WIDTH), lambda b: (0, 0))],
        out_specs=pl.BlockSpec((FNET_S2, r1 * GROUP_WIDTH), lambda b: (b, 0)),
        out_shape=jax.ShapeDtypeStruct((nseq * FNET_S2, r1 * GROUP_WIDTH), BF16),
        scratch_shapes=[pltpu.VMEM((r1, 2 * FNET_S2, GROUP_WIDTH), BF16)],
        compiler_params=_cparams("parallel"),
        name="fnet_mixer",
    )(u, m, cc, sc)
    return out.reshape(nseq * seq, GROUP_WIDTH)


NAT_RB = 8


def _nat_bias_kernel(rpb_ref, onehot_ref, o_ref):
    a = rpb_ref[...]
    a1 = a.astype(BF16)
    r1 = a - a1.astype(F32)
    a2 = r1.astype(BF16)
    a3 = (r1 - a2.astype(F32)).astype(BF16)
    e = onehot_ref[...]
    o_ref[...] = (jnp.dot(a1, e, preferred_element_type=F32) + jnp.dot(a2, e, preferred_element_type=F32)
                  + jnp.dot(a3, e, preferred_element_type=F32))


def _nat_bias_table(rpb, kh):
    n_dr, n_dc = 2 * NAT_KH - 1, 2 * NAT_KW - 1
    c = np.arange(GRID_W)
    dc = np.clip(c[None, :] - c[:, None] + (NAT_KW - 1), 0, n_dc - 1)
    onehot = (np.arange(n_dc)[:, None, None] == dc[None]).astype(np.float32)
    onehot = np.concatenate([onehot.reshape(n_dc, -1), np.zeros((32 - n_dc, GRID_W * GRID_W), np.float32)], 0)
    rows = NAT_HEADS * n_dr
    rows_pad = -(-rows // 8) * 8
    rpb2 = jnp.zeros((rows_pad, 32), F32).at[:rows, :n_dc].set(rpb.reshape(rows, n_dc))
    t = pl.pallas_call(
        _nat_bias_kernel,
        out_shape=jax.ShapeDtypeStruct((rows_pad, GRID_W * GRID_W), F32),
        name="nat_bias",
    )(rpb2, jnp.asarray(onehot, BF16))
    t = t[:rows].reshape(NAT_HEADS, n_dr, GRID_W, GRID_W)
    c0 = np.clip(c - NAT_KW // 2, 0, GRID_W - NAT_KW)
    col_ok = (c[None, :] >= c0[:, None]) & (c[None, :] < c0[:, None] + NAT_KW)
    mask = jnp.asarray(np.where(col_ok, 0.0, NEG), F32)
    tabs = []
    for di in range(kh):
        win = t[:, NAT_KH - 1 - di:NAT_KH - 1 - di + kh]
        win = jnp.transpose(win, (0, 2, 1, 3)) + mask[None, :, None, :]
        tabs.append(win.reshape(NAT_HEADS, GRID_W, kh * GRID_W))
    return jnp.stack(tabs, 0)


def _nat_kernel(q_ref, k_ref, v_ref, bias_ref, o_ref, *, rows, kh):
    rb = pl.program_id(1)

    def row(j, carry):
        r = rb * NAT_RB + j
        kr0 = jnp.clip(r - kh // 2, 0, rows - kh)
        di = r - kr0
        kbase = pl.multiple_of(kr0 * GRID_W, GRID_W)
        kk = k_ref[pl.ds(kbase, kh * GRID_W), :]
        vv = v_ref[pl.ds(kbase, kh * GRID_W), :]
        qbase = pl.multiple_of(j * GRID_W, GRID_W)
        qq = q_ref[pl.ds(qbase, GRID_W), :]
        outs = []
        for h in range(NAT_HEADS):
            sl = slice(h * HEAD_DIM, (h + 1) * HEAD_DIM)
            s = lax.dot_general(qq[:, sl], kk[:, sl], (((1,), (1,)), ((), ())), preferred_element_type=F32)
            s = s * (HEAD_DIM ** -0.5) + bias_ref[di, h]
            m = jnp.max(s, axis=-1, keepdims=True)
            p = jnp.exp(s - m)
            denom = jnp.sum(p, axis=-1, keepdims=True)
            outs.append(jnp.dot(p.astype(BF16), vv[:, sl], preferred_element_type=F32) / denom)
        o_ref[pl.ds(qbase, GRID_W), :] = jnp.concatenate(outs, axis=-1).astype(BF16)
        return carry

    lax.fori_loop(0, NAT_RB, row, 0)


def _nat_mixer(u, bias_tab, nseq, seq):
    rows = seq // GRID_W
    kh = min(NAT_KH, rows)
    nrb = rows // NAT_RB
    tq = NAT_RB * GRID_W
    kv_spec = lambda col: pl.BlockSpec((seq, GROUP_WIDTH), lambda b, i: (b, col // GROUP_WIDTH))
    return pl.pallas_call(
        functools.partial(_nat_kernel, rows=rows, kh=kh),
        grid=(nseq, nrb),
        in_specs=[pl.BlockSpec((tq, GROUP_WIDTH), lambda b, i: (b * nrb + i, COL_NAT_Q // GROUP_WIDTH)),
                  kv_spec(COL_NAT_K), kv_spec(COL_NAT_V),
                  pl.BlockSpec((kh, NAT_HEADS, GRID_W, kh * GRID_W), lambda b, i: (0, 0, 0, 0))],
        out_specs=pl.BlockSpec((tq, GROUP_WIDTH), lambda b, i: (b * nrb + i, 0)),
        out_shape=jax.ShapeDtypeStruct((nseq * seq, GROUP_WIDTH), BF16),
        compiler_params=_cparams("parallel", "arbitrary"),
        name="nat_mixer",
    )(u, u, u, bias_tab)


ROUTE_W = 128


def _out_proj_kernel(ya_ref, yb_ref, yc_ref, yd_ref, x_ref, mg_ref, wo_ref, g2_ref, wr_ref, br_ref, tri_ref,
                     x2_ref, h2_ref, route_ref, count_ref, base_ref):
    @pl.when(pl.program_id(0) == 0)
    def _():
        base_ref[...] = jnp.zeros_like(base_ref)

    acc = x_ref[...]
    for j, y_ref in enumerate((ya_ref, yb_ref, yc_ref, yd_ref)):
        y = y_ref[...].astype(F32)
        y = y * lax.rsqrt(jnp.mean(y * y, axis=-1, keepdims=True) + EPS)
        y = y * mg_ref[:, j * GROUP_WIDTH:(j + 1) * GROUP_WIDTH]
        acc = acc + jnp.dot(y.astype(BF16), wo_ref[j * GROUP_WIDTH:(j + 1) * GROUP_WIDTH, :],
                            preferred_element_type=F32)
    x2_ref[...] = acc
    h2 = acc * lax.rsqrt(jnp.mean(acc * acc, axis=-1, keepdims=True) + EPS) * g2_ref[...]
    h2_ref[...] = h2

    a1 = h2.astype(BF16)
    rem = h2 - a1.astype(F32)
    a2 = rem.astype(BF16)
    a3 = (rem - a2.astype(F32)).astype(BF16)
    w1, w2, w3 = wr_ref[0], wr_ref[1], wr_ref[2]
    dot = lambda a, w: jnp.dot(a, w, preferred_element_type=F32)
    logits = (dot(a1, w1) + (dot(a1, w2) + dot(a2, w1))
              + (dot(a2, w2) + dot(a1, w3) + dot(a3, w1))) + br_ref[...]

    lane = lax.broadcasted_iota(jnp.int32, logits.shape, 1).astype(F32)
    big = jnp.float32(ROUTE_W)
    is_g = lane < N_EXPERT_GROUPS
    gl = jnp.where(is_g, logits, NEG)
    gm = jnp.max(gl, axis=-1, keepdims=True)
    grp = jnp.min(jnp.where(gl == gm, lane, big), axis=-1, keepdims=True)
    gsum = jnp.sum(jnp.where(is_g, jnp.exp(gl - gm), 0.0), axis=-1, keepdims=True)
    p_grp = 1.0 / gsum
    lo = N_EXPERT_GROUPS + EXPERTS_PER_GROUP * grp
    emask = (lane >= lo) & (lane < lo + EXPERTS_PER_GROUP)
    el = jnp.where(emask, logits, NEG)
    l1 = jnp.max(el, axis=-1, keepdims=True)
    i1 = jnp.min(jnp.where(emask & (el == l1), lane, big), axis=-1, keepdims=True)
    rest = emask & (lane != i1)
    el2 = jnp.where(rest, logits, NEG)
    l2 = jnp.max(el2, axis=-1, keepdims=True)
    i2 = jnp.min(jnp.where(rest & (el2 == l2), lane, big), axis=-1, keepdims=True)
    e21 = jnp.exp(l2 - l1)
    gate0 = p_grp / (1.0 + e21)
    gate1 = p_grp * e21 / (1.0 + e21)
    e0 = i1 - N_EXPERT_GROUPS
    e1 = i2 - N_EXPERT_GROUPS

    hit0 = lane == e0
    hit1 = lane == e1
    onehot = jnp.where(hit0 | hit1, 1.0, 0.0)
    before = jnp.dot(tri_ref[...], onehot.astype(BF16), preferred_element_type=F32) + base_ref[0:1, :]
    rank0 = jnp.sum(jnp.where(hit0, before, 0.0), axis=-1, keepdims=True)
    rank1 = jnp.sum(jnp.where(hit1, before, 0.0), axis=-1, keepdims=True)
    new_base = base_ref[0:1, :] + jnp.sum(onehot, axis=0, keepdims=True)
    base_ref[...] = jnp.broadcast_to(new_base, base_ref.shape)
    count_ref[...] = jnp.broadcast_to(new_base, count_ref.shape)

    rec = jnp.zeros_like(logits)
    for col, val in enumerate((e0, e1, gate0, gate1, rank0, rank1)):
        rec = jnp.where(lane == col, val, rec)
    route_ref[...] = rec


def _out_proj(ys, x, mix_g, wo_bf16, g2, wr3, br, tri):
    n = x.shape[0]
    tile = lambda w: pl.BlockSpec((TM, w), lambda i: (i, 0))
    const = lambda *shape: pl.BlockSpec(shape, lambda i: (0,) * len(shape))
    return pl.pallas_call(
        _out_proj_kernel,
        grid=(n // TM,),
        in_specs=[tile(GROUP_WIDTH)] * 4 + [tile(D_MODEL), const(1, D_MODEL), const(D_MODEL, D_MODEL),
                                           const(1, D_MODEL), const(3, D_MODEL, ROUTE_W), const(1, ROUTE_W),
                                           const(TM, TM)],
        out_specs=[tile(D_MODEL), tile(D_MODEL), tile(ROUTE_W), const(8, ROUTE_W)],
        out_shape=[jax.ShapeDtypeStruct((n, D_MODEL), F32), jax.ShapeDtypeStruct((n, D_MODEL), F32),
                   jax.ShapeDtypeStruct((n, ROUTE_W), F32), jax.ShapeDtypeStruct((8, ROUTE_W), F32)],
        scratch_shapes=[pltpu.VMEM((8, ROUTE_W), F32)],
        compiler_params=_cparams("arbitrary"),
        name="out_proj_router",
    )(*ys, x, mix_g.reshape(1, D_MODEL), wo_bf16, g2.reshape(1, D_MODEL), wr3, br, tri)


def _router_weights(wg, bg, we, be):
    w = jnp.concatenate([wg, we, jnp.zeros((D_MODEL, ROUTE_W - N_EXPERT_GROUPS - N_EXPERTS), F32)], -1)
    w1 = w.astype(BF16)
    r = w - w1.astype(F32)
    w2 = r.astype(BF16)
    w3 = (r - w2.astype(F32)).astype(BF16)
    b = jnp.concatenate([bg, be, jnp.zeros((ROUTE_W - N_EXPERT_GROUPS - N_EXPERTS,), F32)]).reshape(1, ROUTE_W)
    return jnp.stack([w1, w2, w3], 0), b


def _moe_kernel(be_ref, idx_hbm, h2_hbm, wg_ref, wu_ref, wd_ref, o_ref, idx_smem, xbuf, idx_sem, row_sem):
    b = pl.program_id(0)
    nb = pl.num_programs(0)

    def idx_copy(blk, slot):
        return pltpu.make_async_copy(idx_hbm.at[pl.ds(blk, 1)], idx_smem.at[pl.ds(slot, 1)], idx_sem.at[slot])

    def row_copy(tok, i, slot):
        return pltpu.make_async_copy(h2_hbm.at[pl.ds(tok, 1)], xbuf.at[slot, pl.ds(i, 1)], row_sem.at[slot])

    def start_rows(slot):
        def body(i, carry):
            row_copy(idx_smem[slot, i], i, slot).start()
            return carry
        lax.fori_loop(0, MOE_BLOCK, body, 0, unroll=8)

    @pl.when(b == 0)
    def _():
        idx_copy(0, 0).start()
        idx_copy(0, 0).wait()
        start_rows(0)

        @pl.when(nb > 1)
        def _():
            idx_copy(1, 1).start()

    cur = b % 2
    nxt = 1 - cur

    @pl.when(b + 1 < nb)
    def _():
        idx_copy(b + 1, nxt).wait()
        start_rows(nxt)

    def wait_body(i, carry):
        row_copy(0, i, cur).wait()
        return carry
    lax.fori_loop(0, MOE_BLOCK, wait_body, 0)

    @pl.when(b + 2 < nb)
    def _():
        idx_copy(b + 2, cur).start()

    x = xbuf[cur].astype(BF16)
    gate = jnp.dot(x, wg_ref[0], preferred_element_type=F32)
    up = jnp.dot(x, wu_ref[0], preferred_element_type=F32)
    hid = (gate / (1.0 + jnp.exp(-gate))) * up
    o_ref[...] = jnp.dot(hid.astype(BF16), wd_ref[0], preferred_element_type=F32)


def _moe_blocks(block_expert, slot_tok, h2, wg, wu, wd):
    nb = block_expert.shape[0]
    wspec = lambda shape: pl.BlockSpec((1,) + shape, lambda b, be: (be[b], 0, 0))
    grid_spec = pltpu.PrefetchScalarGridSpec(
        num_scalar_prefetch=1,
        grid=(nb,),
        in_specs=[pl.BlockSpec(memory_space=pl.ANY), pl.BlockSpec(memory_space=pl.ANY),
                  wspec((D_MODEL, D_EXPERT)), wspec((D_MODEL, D_EXPERT)), wspec((D_EXPERT, D_MODEL))],
        out_specs=pl.BlockSpec((MOE_BLOCK, D_MODEL), lambda b, be: (b, 0)),
        scratch_shapes=[pltpu.SMEM((2, MOE_BLOCK), jnp.int32),
                        pltpu.VMEM((2, MOE_BLOCK, D_MODEL), F32),
                        pltpu.SemaphoreType.DMA((2,)), pltpu.SemaphoreType.DMA((2,))],
    )
    return pl.pallas_call(
        _moe_kernel,
        grid_spec=grid_spec,
        out_shape=jax.ShapeDtypeStruct((nb * MOE_BLOCK, D_MODEL), F32),
        compiler_params=_cparams("arbitrary"),
        name="moe_blocks",
    )(block_expert, slot_tok.reshape(nb, MOE_BLOCK), h2, wg, wu, wd)


def _combine_kernel(dest_hbm, x_ref, route_ref, g_ref, out_hbm, o_ref, idx_smem, ybuf, idx_sem, row_sem,
                    *, final_norm):
    i = pl.program_id(0)
    n_tiles = pl.num_programs(0)

    def idx_copy(tile, slot):
        return pltpu.make_async_copy(dest_hbm.at[pl.ds(tile, 1)], idx_smem.at[pl.ds(slot, 1)], idx_sem.at[slot])

    def row_copy(src, j, slot):
        return pltpu.make_async_copy(out_hbm.at[pl.ds(src, 1)], ybuf.at[slot, pl.ds(j, 1)], row_sem.at[slot])

    def start_rows(slot):
        def body(j, carry):
            row_copy(idx_smem[slot, j], j, slot).start()
            return carry
        lax.fori_loop(0, 2 * TM, body, 0, unroll=8)

    @pl.when(i == 0)
    def _():
        idx_copy(0, 0).start()
        idx_copy(0, 0).wait()
        start_rows(0)

        @pl.when(n_tiles > 1)
        def _():
            idx_copy(1, 1).start()

    cur = i % 2
    nxt = 1 - cur

    @pl.when(i + 1 < n_tiles)
    def _():
        idx_copy(i + 1, nxt).wait()
        start_rows(nxt)

    def wait_body(j, carry):
        row_copy(0, j, cur).wait()
        return carry
    lax.fori_loop(0, 2 * TM, wait_body, 0)

    @pl.when(i + 2 < n_tiles)
    def _():
        idx_copy(i + 2, cur).start()

    route = route_ref[...]
    x = x_ref[...] + route[:, 2:3] * ybuf[cur, 0:TM, :] + route[:, 3:4] * ybuf[cur, TM:2 * TM, :]
    if final_norm:
        x = x * lax.rsqrt(jnp.mean(x * x, axis=-1, keepdims=True) + EPS) * g_ref[...]
    o_ref[...] = x


def _combine(dest, x2, route, out_sorted, final_g, final_norm):
    n = x2.shape[0]
    n_tiles = n // TM
    return pl.pallas_call(
        functools.partial(_combine_kernel, final_norm=final_norm),
        grid=(n_tiles,),
        in_specs=[pl.BlockSpec(memory_space=pl.ANY),
                  pl.BlockSpec((TM, D_MODEL), lambda i: (i, 0)),
                  pl.BlockSpec((TM, ROUTE_W), lambda i: (i, 0)),
                  pl.BlockSpec((1, D_MODEL), lambda i: (0, 0)),
                  pl.BlockSpec(memory_space=pl.ANY)],
        out_specs=pl.BlockSpec((TM, D_MODEL), lambda i: (i, 0)),
        out_shape=jax.ShapeDtypeStruct((n, D_MODEL), F32),
        scratch_shapes=[pltpu.SMEM((2, 2 * TM), jnp.int32),
                        pltpu.VMEM((2, 2 * TM, D_MODEL), F32),
                        pltpu.SemaphoreType.DMA((2,)), pltpu.SemaphoreType.DMA((2,))],
        compiler_params=_cparams("arbitrary"),
        name="moe_combine",
    )(dest, x2, route, final_g.reshape(1, D_MODEL), out_sorted)


def _routing_tables(route, counts, n):
    e = route[:, 0:2].astype(jnp.int32)
    rank = route[:, 4:6].astype(jnp.int32)
    cnt = counts[0, :N_EXPERTS].astype(jnp.int32)
    padded = (cnt + MOE_BLOCK - 1) // MOE_BLOCK * MOE_BLOCK
    pend = jnp.cumsum(padded)
    pstart = pend - padded
    dest = pstart[e] + rank
    n_blocks = -(-(2 * n) // MOE_BLOCK) + N_EXPERTS
    tok = jnp.broadcast_to(jnp.arange(n, dtype=jnp.int32)[:, None], (n, 2))
    slot_tok = jnp.zeros((n_blocks * MOE_BLOCK,), jnp.int32).at[dest.reshape(-1)].set(tok.reshape(-1))
    block_expert = jnp.clip(jnp.searchsorted(pend, jnp.arange(n_blocks, dtype=jnp.int32) * MOE_BLOCK, side='right'),
                            0, N_EXPERTS - 1).astype(jnp.int32)
    dest_tiles = jnp.transpose(dest.reshape(n // TM, TM, 2), (0, 2, 1)).reshape(n // TM, 2 * TM)
    return block_expert, slot_tok, dest_tiles


def _trunk(x, p, nseq, seq):
    n = nseq * seq
    depth = p['w_in'].shape[0]
    rope = _rope_tables(seq)
    fnet_consts = _fnet_constants(seq)
    rows = seq // GRID_W
    kh = min(NAT_KH, rows)
    tri = jnp.asarray(np.tril(np.ones((TM, TM), np.float32), -1), BF16)
    final_g = p['final_norm_g']
    layers = {k: v for k, v in p.items() if k != 'final_norm_g'}

    def layer(x, w):
        u = _in_proj(x, w['norm1_g'], w['w_in'].astype(BF16), rope, seq)
        ya = _conv_mixer(u, w['conv_w'], w['conv_b'], w['conv_norm_g'], w['conv_norm_b'], nseq, seq)
        yb = _swa_mixer(u, w['attn_sink'], nseq, seq)
        yc = _fnet_mixer(u, fnet_consts, nseq, seq)
        yd = _nat_mixer(u, _nat_bias_table(w['nat_rpb'], kh), nseq, seq)
        wr3, br = _router_weights(w['router_group_w'], w['router_group_b'],
                                  w['router_expert_w'], w['router_expert_b'])
        x2, h2, route, counts = _out_proj((ya, yb, yc, yd), x, w['mix_norm_g'], w['w_out'].astype(BF16),
                                          w['norm2_g'], wr3, br, tri)
        block_expert, slot_tok, dest_tiles = _routing_tables(route, counts, n)
        out_sorted = _moe_blocks(block_expert, slot_tok, h2, w['expert_w_gate'].astype(BF16),
                                 w['expert_w_up'].astype(BF16), w['expert_w_down'].astype(BF16))
        return x2, route, dest_tiles, out_sorted

    def body(x, w):
        x2, route, dest_tiles, out_sorted = layer(x, w)
        return _combine(dest_tiles, x2, route, out_sorted, final_g, final_norm=False), None

    head = {k: v[:depth - 1] for k, v in layers.items()}
    last = {k: v[depth - 1] for k, v in layers.items()}
    if depth > 1:
        x, _ = lax.scan(body, x, head)
    x2, route, dest_tiles, out_sorted = layer(x, last)
    return _combine(dest_tiles, x2, route, out_sorted, final_g, final_norm=True)


@jax.jit
def _forward(x_prompt, x_sample, p):
    bp, seq, d = x_prompt.shape
    bs = x_sample.shape[0]
    assert x_sample.shape[1] == seq and d == D_MODEL
    x = jnp.concatenate([x_prompt.reshape(bp * seq, d), x_sample.reshape(bs * seq, d)], 0)
    y = _trunk(x, p, bp + bs, seq)
    return (y[:bp * seq].reshape(bp, seq, d), y[bp * seq:].reshape(bs, seq, d))


def kernel(x_prompt, x_sample, norm1_g, w_in, conv_w, conv_b, conv_norm_g, conv_norm_b, attn_sink, nat_rpb,
           mix_norm_g, w_out, norm2_g, router_group_w, router_group_b, router_expert_w, router_expert_b,
           expert_w_gate, expert_w_up, expert_w_down, final_norm_g):
    p = dict(norm1_g=norm1_g, w_in=w_in, conv_w=conv_w, conv_b=conv_b, conv_norm_g=conv_norm_g,
             conv_norm_b=conv_norm_b, attn_sink=attn_sink, nat_rpb=nat_rpb, mix_norm_g=mix_norm_g, w_out=w_out,
             norm2_g=norm2_g, router_group_w=router_group_w, router_group_b=router_group_b,
             router_expert_w=router_expert_w, router_expert_b=router_expert_b, expert_w_gate=expert_w_gate,
             expert_w_up=expert_w_up, expert_w_down=expert_w_down, final_norm_g=final_norm_g)
    return _forward(x_prompt, x_sample, p)
```

```python
import functools
import math

import numpy as np
import jax
import jax.numpy as jnp
from jax import lax
from jax.experimental import pallas as pl
from jax.experimental.pallas import tpu as pltpu

F32 = jnp.float32
BF16 = jnp.bfloat16

D_MODEL = 1024
GROUP_WIDTH = 256
HEAD_DIM = 64
CONV_WIDTH = 31
CONV_HALO = 16
SWA_HEADS = 4
SWA_KV_HEADS = 2
WINDOW = 128
ROPE_THETA = 500000.0
ROPE_DIM = 16
FNET_CH = 64
FNET_S2 = 256
NAT_HEADS = 4
GRID_W = 64
NAT_KH = 8
NAT_KW = 16
N_EXPERT_GROUPS = 4
EXPERTS_PER_GROUP = 8
N_EXPERTS = 32
D_EXPERT = 512
EPS = 1e-6
IN_WIDTH = 2048
NEG = -1e30

COL_CONV = 0
COL_SWA_Q = 512
COL_SWA_KV = 768
COL_FNET = 1024
COL_NAT_Q = 1280
COL_NAT_K = 1536
COL_NAT_V = 1792

TM = 512
MOE_BLOCK = 256
VMEM_LIMIT = 56 * 1024 * 1024


def _cparams(*sem):
    return pltpu.CompilerParams(dimension_semantics=sem, vmem_limit_bytes=VMEM_LIMIT)


TOK_ROWS = D_MODEL // 128


def _store_token_tiles(ref, x, rows):
    for c in range(TOK_ROWS):
        ref[pl.ds(c, rows, stride=TOK_ROWS), :] = x[:, c * 128:(c + 1) * 128]


def _load_token_tiles(ref, first_row, rows):
    return jnp.concatenate([ref[pl.ds(first_row + c, rows, stride=TOK_ROWS), :] for c in range(TOK_ROWS)], axis=-1)


def _split_dot(a, m_bf16):
    hi = a.astype(BF16)
    lo = (a - hi.astype(F32)).astype(BF16)
    return (jnp.dot(hi, m_bf16, preferred_element_type=F32)
            + jnp.dot(lo, m_bf16, preferred_element_type=F32))


def _in_proj_kernel(x_ref, g_ref, w_ref, cos_ref, sina_ref, sinb_ref, u_ref):
    x = x_ref[...]
    h = x * lax.rsqrt(jnp.mean(x * x, axis=-1, keepdims=True) + EPS) * g_ref[...]
    u = jnp.dot(h.astype(BF16), w_ref[...], preferred_element_type=F32)
    u_ref[:, :COL_SWA_Q] = u[:, :COL_SWA_Q].astype(BF16)
    cosf, sina, sinb = cos_ref[...], sina_ref[...], sinb_ref[...]
    for c in range(COL_SWA_Q, COL_SWA_KV + 128, 128):
        t = u[:, c:c + 128]
        fwd = pltpu.roll(t, 128 - ROPE_DIM // 2, axis=1)
        bwd = pltpu.roll(t, ROPE_DIM // 2, axis=1)
        u_ref[:, c:c + 128] = (t * cosf + fwd * sina + bwd * sinb).astype(BF16)
    u_ref[:, COL_SWA_KV + 128:] = u[:, COL_SWA_KV + 128:].astype(BF16)


def _in_proj(x, g, w_bf16, rope, seq):
    n = x.shape[0]
    tiles_per_seq = seq // TM
    cosf, sina, sinb = rope
    rope_spec = pl.BlockSpec((TM, 128), lambda i: (i % tiles_per_seq, 0))
    return pl.pallas_call(
        _in_proj_kernel,
        grid=(n // TM,),
        in_specs=[pl.BlockSpec((TM, D_MODEL), lambda i: (i, 0)),
                  pl.BlockSpec((1, D_MODEL), lambda i: (0, 0)),
                  pl.BlockSpec((D_MODEL, IN_WIDTH), lambda i: (0, 0)),
                  rope_spec, rope_spec, rope_spec],
        out_specs=pl.BlockSpec((TM, IN_WIDTH), lambda i: (i, 0)),
        out_shape=jax.ShapeDtypeStruct((n, IN_WIDTH), BF16),
        compiler_params=_cparams("parallel"),
        name="in_proj",
    )(x, g.reshape(1, D_MODEL), w_bf16, cosf, sina, sinb)


def _rope_tables(seq):
    half = ROPE_DIM // 2
    inv = jnp.float32(ROPE_THETA) ** (-jnp.arange(half, dtype=F32) * 2.0 / ROPE_DIM)
    ang = jnp.arange(seq).astype(F32)[:, None] * inv[None, :]
    cos, sin = jnp.cos(ang), jnp.sin(ang)
    ones = jnp.ones((seq, HEAD_DIM - ROPE_DIM), F32)
    zeros8 = jnp.zeros((seq, half), F32)
    zeros48 = jnp.zeros((seq, HEAD_DIM - ROPE_DIM), F32)
    cosf = jnp.concatenate([cos, cos, ones], -1)
    sina = jnp.concatenate([-sin, zeros8, zeros48], -1)
    sinb = jnp.concatenate([zeros8, sin, zeros48], -1)
    return tuple(jnp.tile(t, (1, 128 // HEAD_DIM)) for t in (cosf, sina, sinb))


CONV_CHUNK = 128


def _conv_kernel(u_ref, w_ref, b_ref, ng_ref, nb_ref, avg_ref, o_ref, vpad_ref, *, seq):
    zeros = jnp.zeros((CONV_HALO, GROUP_WIDTH), F32)
    vpad_ref[0:CONV_HALO, :] = zeros
    vpad_ref[CONV_HALO + seq:CONV_HALO + seq + CONV_HALO, :] = zeros

    def glu(i, carry):
        r = pl.multiple_of(i * CONV_CHUNK, CONV_CHUNK)
        blk = u_ref[pl.ds(r, CONV_CHUNK), :].astype(F32)
        a, g = blk[:, :GROUP_WIDTH], blk[:, GROUP_WIDTH:]
        vpad_ref[pl.ds(CONV_HALO + r, CONV_CHUNK), :] = a / (1.0 + jnp.exp(-g))
        return carry

    lax.fori_loop(0, seq // CONV_CHUNK, glu, 0)
    avg = avg_ref[...]
    bias, ng, nb = b_ref[...], ng_ref[...], nb_ref[...]

    def conv(i, carry):
        r = pl.multiple_of(i * CONV_CHUNK, CONV_CHUNK)
        acc = jnp.zeros((CONV_CHUNK, GROUP_WIDTH), F32) + bias
        win_rows = CONV_CHUNK + 2 * CONV_HALO
        win = vpad_ref[pl.ds(r, win_rows), :]
        for b in range(8):
            shifted = win if b == 0 else pltpu.roll(win, win_rows - b, axis=0)
            for a in range(4):
                j = 8 * a + b - (CONV_HALO - CONV_WIDTH // 2)
                if 0 <= j < CONV_WIDTH:
                    acc = acc + w_ref[j:j + 1, :] * shifted[8 * a:8 * a + CONV_CHUNK, :]
        mu = _split_dot(acc, avg)
        d = acc - mu
        var = _split_dot(d * d, avg)
        y = d * lax.rsqrt(var + EPS) * ng + nb
        o_ref[pl.ds(r, CONV_CHUNK), :] = (y / (1.0 + jnp.exp(-y))).astype(BF16)
        return carry

    lax.fori_loop(0, seq // CONV_CHUNK, conv, 0)


def _conv_mixer(u, conv_w, conv_b, norm_g, norm_b, nseq, seq):
    gid = np.arange(GROUP_WIDTH) // HEAD_DIM
    avg = jnp.asarray((gid[:, None] == gid[None, :]).astype(np.float32) / HEAD_DIM, BF16)
    row = lambda v: v.reshape(1, GROUP_WIDTH)
    const = lambda shape: pl.BlockSpec(shape, lambda b: (0, 0))
    return pl.pallas_call(
        functools.partial(_conv_kernel, seq=seq),
        grid=(nseq,),
        in_specs=[pl.BlockSpec((seq, 2 * GROUP_WIDTH), lambda b: (b, COL_CONV // (2 * GROUP_WIDTH))),
                  const((CONV_WIDTH, GROUP_WIDTH)), const((1, GROUP_WIDTH)), const((1, GROUP_WIDTH)),
                  const((1, GROUP_WIDTH)), const((GROUP_WIDTH, GROUP_WIDTH))],
        out_specs=pl.BlockSpec((seq, GROUP_WIDTH), lambda b: (b, 0)),
        out_shape=jax.ShapeDtypeStruct((nseq * seq, GROUP_WIDTH), BF16),
        scratch_shapes=[pltpu.VMEM((seq + 2 * CONV_HALO, GROUP_WIDTH), F32)],
        compiler_params=_cparams("parallel"),
        name="conv_mixer",
    )(u, conv_w, row(conv_b), row(norm_g), row(norm_b), avg)


SWA_TQ = 256


def _swa_kernel(sink_ref, q_ref, kv_ref, o_ref, *, seq):
    i = pl.program_id(1)
    kw = SWA_TQ + 2 * WINDOW
    kstart = pl.multiple_of(jnp.clip(i * SWA_TQ - WINDOW, 0, seq - kw), WINDOW)
    kv = kv_ref[pl.ds(kstart, kw), :]
    q = q_ref[...]
    qpos = i * SWA_TQ + lax.broadcasted_iota(jnp.int32, (SWA_TQ, kw), 0)
    kpos = kstart + lax.broadcasted_iota(jnp.int32, (SWA_TQ, kw), 1)
    valid = jnp.abs(qpos - kpos) <= WINDOW
    rep = SWA_HEADS // SWA_KV_HEADS
    for g in range(SWA_KV_HEADS):
        k = kv[:, g * HEAD_DIM:(g + 1) * HEAD_DIM]
        v = kv[:, SWA_KV_HEADS * HEAD_DIM + g * HEAD_DIM:SWA_KV_HEADS * HEAD_DIM + (g + 1) * HEAD_DIM]
        for r in range(rep):
            h = g * rep + r
            qh = q[:, h * HEAD_DIM:(h + 1) * HEAD_DIM]
            s = lax.dot_general(qh, k, (((1,), (1,)), ((), ())), preferred_element_type=F32)
            s = jnp.where(valid, s * (HEAD_DIM ** -0.5), NEG)
            sink = sink_ref[h]
            m = jnp.maximum(jnp.max(s, axis=-1, keepdims=True), sink)
            p = jnp.exp(s - m)
            denom = jnp.sum(p, axis=-1, keepdims=True) + jnp.exp(sink - m)
            o = jnp.dot(p.astype(BF16), v, preferred_element_type=F32) / denom
            o_ref[:, h * HEAD_DIM:(h + 1) * HEAD_DIM] = o.astype(BF16)


def _swa_mixer(u, sink, nseq, seq):
    nq = seq // SWA_TQ
    return pl.pallas_call(
        functools.partial(_swa_kernel, seq=seq),
        grid=(nseq, nq),
        in_specs=[pl.BlockSpec(memory_space=pltpu.SMEM),
                  pl.BlockSpec((SWA_TQ, GROUP_WIDTH), lambda b, i: (b * nq + i, COL_SWA_Q // GROUP_WIDTH)),
                  pl.BlockSpec((seq, GROUP_WIDTH), lambda b, i: (b, COL_SWA_KV // GROUP_WIDTH))],
        out_specs=pl.BlockSpec((SWA_TQ, GROUP_WIDTH), lambda b, i: (b * nq + i, 0)),
        out_shape=jax.ShapeDtypeStruct((nseq * seq, GROUP_WIDTH), BF16),
        compiler_params=_cparams("parallel", "arbitrary"),
        name="swa_mixer",
    )(sink, u, u)


def _fnet_kernel(x_ref, m_ref, cc_ref, sc_ref, o_ref, z_ref, *, seq):
    r1 = seq // FNET_S2
    half = r1 // 2
    xs = [x_ref[s1 * FNET_S2:(s1 + 1) * FNET_S2, :].astype(F32) for s1 in range(r1)]
    sym = {s1: xs[s1] + xs[r1 - s1] for s1 in range(1, half)}
    asym = {s1: xs[s1] - xs[r1 - s1] for s1 in range(1, half)}

    def weighted(terms):
        acc = None
        for w, t in terms:
            if abs(w) < 1e-9:
                continue
            if abs(w - 1.0) < 1e-9:
                acc = t if acc is None else acc + t
            elif abs(w + 1.0) < 1e-9:
                acc = -t if acc is None else acc - t
            else:
                acc = w * t if acc is None else acc + w * t
        return jnp.zeros((FNET_S2, GROUP_WIDTH), F32) if acc is None else acc

    for k1 in range(half + 1):
        re_terms = [(1.0, xs[0]), (float((-1) ** k1), xs[half])]
        re_terms += [(math.cos(2 * math.pi * k1 * s1 / r1), sym[s1]) for s1 in range(1, half)]
        im_terms = [(-math.sin(2 * math.pi * k1 * s1 / r1), asym[s1]) for s1 in range(1, half)]
        zr = weighted(re_terms).astype(BF16)
        zi = weighted(im_terms)
        z_ref[k1, :FNET_S2, :] = zr
        z_ref[k1, FNET_S2:, :] = zi.astype(BF16)
        if 0 < k1 < half:
            z_ref[r1 - k1, :FNET_S2, :] = zr
            z_ref[r1 - k1, FNET_S2:, :] = (-zi).astype(BF16)

    scale = 1.0 / math.sqrt(seq * FNET_CH)
    cc, sc = cc_ref[...], sc_ref[...]
    for k1 in range(r1):
        y = jnp.dot(m_ref[k1], z_ref[k1], preferred_element_type=F32)
        out = (jnp.dot(y[:FNET_S2].astype(BF16), cc, preferred_element_type=F32)
               + jnp.dot(y[FNET_S2:].astype(BF16), sc, preferred_element_type=F32))
        o_ref[:, k1 * GROUP_WIDTH:(k1 + 1) * GROUP_WIDTH] = (out * scale).astype(BF16)


def _fnet_constants(seq):
    r1 = seq // FNET_S2
    k = (np.arange(r1)[:, None, None] + r1 * np.arange(FNET_S2)[None, :, None]).astype(np.float64)
    s2 = np.arange(FNET_S2)[None, None, :].astype(np.float64)
    ang = 2 * np.pi * ((k * s2) % seq) / seq
    c, s = np.cos(ang), np.sin(ang)
    m = np.concatenate([np.concatenate([c, s], -1), np.concatenate([-s, c], -1)], 1)
    ch = np.arange(GROUP_WIDTH)
    same = (ch[:, None] // FNET_CH) == (ch[None, :] // FNET_CH)
    ang_c = 2 * np.pi * ((ch[:, None] % FNET_CH) * (ch[None, :] % FNET_CH) % FNET_CH) / FNET_CH
    cc = np.where(same, np.cos(ang_c), 0.0)
    sc = np.where(same, np.sin(ang_c), 0.0)
    return (jnp.asarray(m, BF16), jnp.asarray(cc, BF16), jnp.asarray(sc, BF16))


def _fnet_mixer(u, consts, nseq, seq):
    r1 = seq // FNET_S2
    m, cc, sc = consts
    out = pl.pallas_call(
        functools.partial(_fnet_kernel, seq=seq),
        grid=(nseq,),
        in_specs=[pl.BlockSpec((seq, GROUP_WIDTH), lambda b: (b, COL_FNET // GROUP_WIDTH)),
                  pl.BlockSpec((r1, 2 * FNET_S2, 2 * FNET_S2), lambda b: (0, 0, 0)),
                  pl.BlockSpec((GROUP_WIDTH, GROUP_WIDTH), lambda b: (0, 0)),
                  pl.BlockSpec((GROUP_WIDTH, GROUP_WIDTH), lambda b: (0, 0))],
        out_specs=pl.BlockSpec((FNET_S2, r1 * GROUP_WIDTH), lambda b: (b, 0)),
        out_shape=jax.ShapeDtypeStruct((nseq * FNET_S2, r1 * GROUP_WIDTH), BF16),
        scratch_shapes=[pltpu.VMEM((r1, 2 * FNET_S2, GROUP_WIDTH), BF16)],
        compiler_params=_cparams("parallel"),
        name="fnet_mixer",
    )(u, m, cc, sc)
    return out.reshape(nseq * seq, GROUP_WIDTH)


NAT_G = 4


def _nat_geometry(rows, kh):
    wr = NAT_G + kh
    n_groups = rows // NAT_G
    w0 = np.clip(NAT_G * np.arange(n_groups) - kh // 2, 0, rows - wr)
    patterns, pat_id = [], []
    for g in range(n_groups):
        r = NAT_G * g + np.arange(NAT_G)
        kr0 = np.clip(r - kh // 2, 0, rows - kh)
        kr = w0[g] + np.arange(wr)
        valid = (kr[None, :] >= kr0[:, None]) & (kr[None, :] < kr0[:, None] + kh)
        dr = np.where(valid, kr[None, :] - r[:, None] + NAT_KH - 1, 0)
        assert valid.sum(1).min() == kh and dr.min() >= 0 and dr.max() <= 2 * NAT_KH - 2
        for i, (v, d) in enumerate(patterns):
            if np.array_equal(v, valid) and np.array_equal(d, dr):
                pat_id.append(i)
                break
        else:
            pat_id.append(len(patterns))
            patterns.append((valid, dr))
    return w0.astype(np.int32), np.asarray(pat_id, np.int32), patterns


def _nat_bias_kernel(rpb_ref, onehot_ref, o_ref):
    a = rpb_ref[...]
    a1 = a.astype(BF16)
    r1 = a - a1.astype(F32)
    a2 = r1.astype(BF16)
    a3 = (r1 - a2.astype(F32)).astype(BF16)
    e = onehot_ref[...]
    o_ref[...] = (jnp.dot(a1, e, preferred_element_type=F32) + jnp.dot(a2, e, preferred_element_type=F32)
                  + jnp.dot(a3, e, preferred_element_type=F32))


def _nat_bias_table(rpb, patterns):
    n_dr, n_dc = 2 * NAT_KH - 1, 2 * NAT_KW - 1
    c = np.arange(GRID_W)
    dc = np.clip(c[None, :] - c[:, None] + (NAT_KW - 1), 0, n_dc - 1)
    onehot = (np.arange(n_dc)[:, None, None] == dc[None]).astype(np.float32)
    onehot = np.concatenate([onehot.reshape(n_dc, -1), np.zeros((32 - n_dc, GRID_W * GRID_W), np.float32)], 0)
    rows = NAT_HEADS * n_dr
    rows_pad = -(-rows // 8) * 8
    rpb2 = jnp.zeros((rows_pad, 32), F32).at[:rows, :n_dc].set(rpb.reshape(rows, n_dc))
    t = pl.pallas_call(
        _nat_bias_kernel,
        out_shape=jax.ShapeDtypeStruct((rows_pad, GRID_W * GRID_W), F32),
        name="nat_bias",
    )(rpb2, jnp.asarray(onehot, BF16))
    t = t[:rows].reshape(NAT_HEADS, n_dr, GRID_W, GRID_W)
    c0 = np.clip(c - NAT_KW // 2, 0, GRID_W - NAT_KW)
    col_ok = (c[None, :] >= c0[:, None]) & (c[None, :] < c0[:, None] + NAT_KW)
    mask = jnp.asarray(np.where(col_ok, 0.0, NEG), F32)
    tabs = []
    for valid, dr in patterns:
        wr = valid.shape[1]
        win = t[:, dr] + mask[None, None, None]
        win = jnp.where(jnp.asarray(valid)[None, :, :, None, None], win, NEG)
        win = jnp.transpose(win, (0, 1, 3, 2, 4))
        tabs.append(win.reshape(NAT_HEADS, NAT_G * GRID_W, wr * GRID_W))
    return jnp.stack(tabs, 0)


def _nat_kernel(w0_ref, pat_ref, q_ref, k_ref, v_ref, bias_ref, o_ref, *, wk):
    g = pl.program_id(1)
    kbase = pl.multiple_of(w0_ref[g] * GRID_W, GRID_W)
    kk = k_ref[pl.ds(kbase, wk), :]
    vv = v_ref[pl.ds(kbase, wk), :]
    q = q_ref[...]
    tq = q.shape[0]
    lane = lax.broadcasted_iota(jnp.int32, (tq, 128), 1)
    upper, lower = lane >= HEAD_DIM, lane < HEAD_DIM
    for hp in range(NAT_HEADS // 2):
        sl = slice(hp * 128, (hp + 1) * 128)
        q2, k2, v2 = q[:, sl], kk[:, sl], vv[:, sl]
        outs = []
        for r in range(2):
            qm = jnp.where(upper if r else lower, q2, jnp.zeros_like(q2))
            s = lax.dot_general(qm, k2, (((1,), (1,)), ((), ())), preferred_element_type=F32)
            s = s * (HEAD_DIM ** -0.5) + bias_ref[0, 2 * hp + r]
            m = jnp.max(s, axis=-1, keepdims=True)
            p = jnp.exp(s - m)
            denom = jnp.sum(p, axis=-1, keepdims=True)
            outs.append(jnp.dot(p.astype(BF16), v2, preferred_element_type=F32) / denom)
        o_ref[:, sl] = jnp.where(upper, outs[1], outs[0]).astype(BF16)


def _nat_mixer(u, rpb, nseq, seq):
    rows = seq // GRID_W
    kh = min(NAT_KH, rows)
    w0, pat_id, patterns = _nat_geometry(rows, kh)
    bias_tab = _nat_bias_table(rpb, patterns)
    n_groups = rows // NAT_G
    tq = NAT_G * GRID_W
    wk = (NAT_G + kh) * GRID_W
    kv_spec = lambda col: pl.BlockSpec((seq, GROUP_WIDTH), lambda b, g, w0, pat: (b, col // GROUP_WIDTH))
    grid_spec = pltpu.PrefetchScalarGridSpec(
        num_scalar_prefetch=2,
        grid=(nseq, n_groups),
        in_specs=[pl.BlockSpec((tq, GROUP_WIDTH),
                               lambda b, g, w0, pat: (b * n_groups + g, COL_NAT_Q // GROUP_WIDTH)),
                  kv_spec(COL_NAT_K), kv_spec(COL_NAT_V),
                  pl.BlockSpec((1, NAT_HEADS, tq, wk), lambda b, g, w0, pat: (pat[g], 0, 0, 0))],
        out_specs=pl.BlockSpec((tq, GROUP_WIDTH), lambda b, g, w0, pat: (b * n_groups + g, 0)),
    )
    return pl.pallas_call(
        functools.partial(_nat_kernel, wk=wk),
        grid_spec=grid_spec,
        out_shape=jax.ShapeDtypeStruct((nseq * seq, GROUP_WIDTH), BF16),
        compiler_params=_cparams("parallel", "arbitrary"),
        name="nat_mixer",
    )(jnp.asarray(w0), jnp.asarray(pat_id), u, u, u, bias_tab)


ROUTE_W = 128


def _out_proj_kernel(ya_ref, yb_ref, yc_ref, yd_ref, x_ref, mg_ref, wo_ref, g2_ref, wr_ref, br_ref, tri_ref,
                     x2_ref, h2_ref, route_ref, count_ref, base_ref):
    @pl.when(pl.program_id(0) == 0)
    def _():
        base_ref[...] = jnp.zeros_like(base_ref)

    acc = x_ref[...]
    for j, y_ref in enumerate((ya_ref, yb_ref, yc_ref, yd_ref)):
        y = y_ref[...].astype(F32)
        y = y * lax.rsqrt(jnp.mean(y * y, axis=-1, keepdims=True) + EPS)
        y = y * mg_ref[:, j * GROUP_WIDTH:(j + 1) * GROUP_WIDTH]
        acc = acc + jnp.dot(y.astype(BF16), wo_ref[j * GROUP_WIDTH:(j + 1) * GROUP_WIDTH, :],
                            preferred_element_type=F32)
    x2_ref[...] = acc
    h2 = acc * lax.rsqrt(jnp.mean(acc * acc, axis=-1, keepdims=True) + EPS) * g2_ref[...]
    _store_token_tiles(h2_ref, h2, TM)

    a1 = h2.astype(BF16)
    rem = h2 - a1.astype(F32)
    a2 = rem.astype(BF16)
    a3 = (rem - a2.astype(F32)).astype(BF16)
    w1, w2, w3 = wr_ref[0], wr_ref[1], wr_ref[2]
    dot = lambda a, w: jnp.dot(a, w, preferred_element_type=F32)
    logits = (dot(a1, w1) + (dot(a1, w2) + dot(a2, w1))
              + (dot(a2, w2) + dot(a1, w3) + dot(a3, w1))) + br_ref[...]

    lane = lax.broadcasted_iota(jnp.int32, logits.shape, 1).astype(F32)
    big = jnp.float32(ROUTE_W)
    is_g = lane < N_EXPERT_GROUPS
    gl = jnp.where(is_g, logits, NEG)
    gm = jnp.max(gl, axis=-1, keepdims=True)
    grp = jnp.min(jnp.where(gl == gm, lane, big), axis=-1, keepdims=True)
    gsum = jnp.sum(jnp.where(is_g, jnp.exp(gl - gm), 0.0), axis=-1, keepdims=True)
    p_grp = 1.0 / gsum
    lo = N_EXPERT_GROUPS + EXPERTS_PER_GROUP * grp
    emask = (lane >= lo) & (lane < lo + EXPERTS_PER_GROUP)
    el = jnp.where(emask, logits, NEG)
    l1 = jnp.max(el, axis=-1, keepdims=True)
    i1 = jnp.min(jnp.where(emask & (el == l1), lane, big), axis=-1, keepdims=True)
    rest = emask & (lane != i1)
    el2 = jnp.where(rest, logits, NEG)
    l2 = jnp.max(el2, axis=-1, keepdims=True)
    i2 = jnp.min(jnp.where(rest & (el2 == l2), lane, big), axis=-1, keepdims=True)
    e21 = jnp.exp(l2 - l1)
    gate0 = p_grp / (1.0 + e21)
    gate1 = p_grp * e21 / (1.0 + e21)
    e0 = i1 - N_EXPERT_GROUPS
    e1 = i2 - N_EXPERT_GROUPS

    hit0 = lane == e0
    hit1 = lane == e1
    onehot = jnp.where(hit0 | hit1, 1.0, 0.0)
    before = jnp.dot(tri_ref[...], onehot.astype(BF16), preferred_element_type=F32) + base_ref[0:1, :]
    rank0 = jnp.sum(jnp.where(hit0, before, 0.0), axis=-1, keepdims=True)
    rank1 = jnp.sum(jnp.where(hit1, before, 0.0), axis=-1, keepdims=True)
    new_base = base_ref[0:1, :] + jnp.sum(onehot, axis=0, keepdims=True)
    base_ref[...] = jnp.broadcast_to(new_base, base_ref.shape)
    count_ref[...] = jnp.broadcast_to(new_base, count_ref.shape)

    rec = jnp.zeros_like(logits)
    for col, val in enumerate((e0, e1, gate0, gate1, rank0, rank1)):
        rec = jnp.where(lane == col, val, rec)
    route_ref[...] = rec


def _out_proj(ys, x, mix_g, wo_bf16, g2, wr3, br, tri):
    n = x.shape[0]
    tile = lambda w: pl.BlockSpec((TM, w), lambda i: (i, 0))
    const = lambda *shape: pl.BlockSpec(shape, lambda i: (0,) * len(shape))
    return pl.pallas_call(
        _out_proj_kernel,
        grid=(n // TM,),
        in_specs=[tile(GROUP_WIDTH)] * 4 + [tile(D_MODEL), const(1, D_MODEL), const(D_MODEL, D_MODEL),
                                           const(1, D_MODEL), const(3, D_MODEL, ROUTE_W), const(1, ROUTE_W),
                                           const(TM, TM)],
        out_specs=[tile(D_MODEL), pl.BlockSpec((TM * TOK_ROWS, 128), lambda i: (i, 0)), tile(ROUTE_W),
                   const(8, ROUTE_W)],
        out_shape=[jax.ShapeDtypeStruct((n, D_MODEL), F32), jax.ShapeDtypeStruct((n * TOK_ROWS, 128), F32),
                   jax.ShapeDtypeStruct((n, ROUTE_W), F32), jax.ShapeDtypeStruct((8, ROUTE_W), F32)],
        scratch_shapes=[pltpu.VMEM((8, ROUTE_W), F32)],
        compiler_params=_cparams("arbitrary"),
        name="out_proj_router",
    )(*ys, x, mix_g.reshape(1, D_MODEL), wo_bf16, g2.reshape(1, D_MODEL), wr3, br, tri)


def _router_weights(wg, bg, we, be):
    w = jnp.concatenate([wg, we, jnp.zeros((D_MODEL, ROUTE_W - N_EXPERT_GROUPS - N_EXPERTS), F32)], -1)
    w1 = w.astype(BF16)
    r = w - w1.astype(F32)
    w2 = r.astype(BF16)
    w3 = (r - w2.astype(F32)).astype(BF16)
    b = jnp.concatenate([bg, be, jnp.zeros((ROUTE_W - N_EXPERT_GROUPS - N_EXPERTS,), F32)]).reshape(1, ROUTE_W)
    return jnp.stack([w1, w2, w3], 0), b


def _moe_kernel(be_ref, idx_hbm, h2_hbm, wg_ref, wu_ref, wd_ref, o_ref, idx_smem, xbuf, idx_sem, row_sem):
    b = pl.program_id(0)
    nb = pl.num_programs(0)

    def idx_copy(blk, slot):
        return pltpu.make_async_copy(idx_hbm.at[pl.ds(blk, 1)], idx_smem.at[pl.ds(slot, 1)], idx_sem.at[slot])

    def row_copy(tok, i, slot):
        src = h2_hbm.at[pl.ds(pl.multiple_of(tok * TOK_ROWS, TOK_ROWS), TOK_ROWS)]
        dst = xbuf.at[slot, pl.ds(pl.multiple_of(i * TOK_ROWS, TOK_ROWS), TOK_ROWS)]
        return pltpu.make_async_copy(src, dst, row_sem.at[slot])

    def start_rows(slot):
        def body(i, carry):
            row_copy(idx_smem[slot, i], i, slot).start()
            return carry
        lax.fori_loop(0, MOE_BLOCK, body, 0, unroll=8)

    @pl.when(b == 0)
    def _():
        idx_copy(0, 0).start()
        idx_copy(0, 0).wait()
        start_rows(0)

        @pl.when(nb > 1)
        def _():
            idx_copy(1, 1).start()

    cur = b % 2
    nxt = 1 - cur

    @pl.when(b + 1 < nb)
    def _():
        idx_copy(b + 1, nxt).wait()
        start_rows(nxt)

    def wait_body(i, carry):
        row_copy(0, i, cur).wait()
        return carry
    lax.fori_loop(0, MOE_BLOCK, wait_body, 0)

    @pl.when(b + 2 < nb)
    def _():
        idx_copy(b + 2, cur).start()

    x = _load_token_tiles(xbuf.at[cur], 0, MOE_BLOCK).astype(BF16)
    gate = jnp.dot(x, wg_ref[0], preferred_element_type=F32)
    up = jnp.dot(x, wu_ref[0], preferred_element_type=F32)
    hid = (gate / (1.0 + jnp.exp(-gate))) * up
    _store_token_tiles(o_ref, jnp.dot(hid.astype(BF16), wd_ref[0], preferred_element_type=F32), MOE_BLOCK)


def _moe_blocks(block_expert, slot_tok, h2, wg, wu, wd):
    nb = block_expert.shape[0]
    wspec = lambda shape: pl.BlockSpec((1,) + shape, lambda b, be: (be[b], 0, 0))
    grid_spec = pltpu.PrefetchScalarGridSpec(
        num_scalar_prefetch=1,
        grid=(nb,),
        in_specs=[pl.BlockSpec(memory_space=pl.ANY), pl.BlockSpec(memory_space=pl.ANY),
                  wspec((D_MODEL, D_EXPERT)), wspec((D_MODEL, D_EXPERT)), wspec((D_EXPERT, D_MODEL))],
        out_specs=pl.BlockSpec((MOE_BLOCK * TOK_ROWS, 128), lambda b, be: (b, 0)),
        scratch_shapes=[pltpu.SMEM((2, MOE_BLOCK), jnp.int32),
                        pltpu.VMEM((2, MOE_BLOCK * TOK_ROWS, 128), F32),
                        pltpu.SemaphoreType.DMA((2,)), pltpu.SemaphoreType.DMA((2,))],
    )
    return pl.pallas_call(
        _moe_kernel,
        grid_spec=grid_spec,
        out_shape=jax.ShapeDtypeStruct((nb * MOE_BLOCK * TOK_ROWS, 128), F32),
        compiler_params=_cparams("arbitrary"),
        name="moe_blocks",
    )(block_expert, slot_tok.reshape(nb, MOE_BLOCK), h2, wg, wu, wd)


def _combine_kernel(dest_hbm, x_ref, route_ref, g_ref, out_hbm, o_ref, idx_smem, ybuf, idx_sem, row_sem,
                    *, final_norm):
    i = pl.program_id(0)
    n_tiles = pl.num_programs(0)

    def idx_copy(tile, slot):
        return pltpu.make_async_copy(dest_hbm.at[pl.ds(tile, 1)], idx_smem.at[pl.ds(slot, 1)], idx_sem.at[slot])

    def row_copy(src, j, slot):
        src_tile = out_hbm.at[pl.ds(pl.multiple_of(src * TOK_ROWS, TOK_ROWS), TOK_ROWS)]
        dst_tile = ybuf.at[slot, pl.ds(pl.multiple_of(j * TOK_ROWS, TOK_ROWS), TOK_ROWS)]
        return pltpu.make_async_copy(src_tile, dst_tile, row_sem.at[slot])

    def start_rows(slot):
        def body(j, carry):
            row_copy(idx_smem[slot, j], j, slot).start()
            return carry
        lax.fori_loop(0, 2 * TM, body, 0, unroll=8)

    @pl.when(i == 0)
    def _():
        idx_copy(0, 0).start()
        idx_copy(0, 0).wait()
        start_rows(0)

        @pl.when(n_tiles > 1)
        def _():
            idx_copy(1, 1).start()

    cur = i % 2
    nxt = 1 - cur

    @pl.when(i + 1 < n_tiles)
    def _():
        idx_copy(i + 1, nxt).wait()
        start_rows(nxt)

    def wait_body(j, carry):
        row_copy(0, j, cur).wait()
        return carry
    lax.fori_loop(0, 2 * TM, wait_body, 0)

    @pl.when(i + 2 < n_tiles)
    def _():
        idx_copy(i + 2, cur).start()

    route = route_ref[...]
    y0 = _load_token_tiles(ybuf.at[cur], 0, TM)
    y1 = _load_token_tiles(ybuf.at[cur], TM * TOK_ROWS, TM)
    x = x_ref[...] + route[:, 2:3] * y0 + route[:, 3:4] * y1
    if final_norm:
        x = x * lax.rsqrt(jnp.mean(x * x, axis=-1, keepdims=True) + EPS) * g_ref[...]
    o_ref[...] = x


def _combine(dest, x2, route, out_sorted, final_g, final_norm):
    n = x2.shape[0]
    n_tiles = n // TM
    return pl.pallas_call(
        functools.partial(_combine_kernel, final_norm=final_norm),
        grid=(n_tiles,),
        in_specs=[pl.BlockSpec(memory_space=pl.ANY),
                  pl.BlockSpec((TM, D_MODEL), lambda i: (i, 0)),
                  pl.BlockSpec((TM, ROUTE_W), lambda i: (i, 0)),
                  pl.BlockSpec((1, D_MODEL), lambda i: (0, 0)),
                  pl.BlockSpec(memory_space=pl.ANY)],
        out_specs=pl.BlockSpec((TM, D_MODEL), lambda i: (i, 0)),
        out_shape=jax.ShapeDtypeStruct((n, D_MODEL), F32),
        scratch_shapes=[pltpu.SMEM((2, 2 * TM), jnp.int32),
                        pltpu.VMEM((2, 2 * TM * TOK_ROWS, 128), F32),
                        pltpu.SemaphoreType.DMA((2,)), pltpu.SemaphoreType.DMA((2,))],
        compiler_params=_cparams("arbitrary"),
        name="moe_combine",
    )(dest, x2, route, final_g.reshape(1, D_MODEL), out_sorted)


def _routing_tables(route, counts, n):
    e = route[:, 0:2].astype(jnp.int32)
    rank = route[:, 4:6].astype(jnp.int32)
    cnt = counts[0, :N_EXPERTS].astype(jnp.int32)
    padded = (cnt + MOE_BLOCK - 1) // MOE_BLOCK * MOE_BLOCK
    pend = jnp.cumsum(padded)
    pstart = pend - padded
    dest = pstart[e] + rank
    n_blocks = -(-(2 * n) // MOE_BLOCK) + N_EXPERTS
    tok = jnp.broadcast_to(jnp.arange(n, dtype=jnp.int32)[:, None], (n, 2))
    slot_tok = jnp.zeros((n_blocks * MOE_BLOCK,), jnp.int32).at[dest.reshape(-1)].set(tok.reshape(-1))
    block_first = jnp.arange(n_blocks, dtype=jnp.int32) * MOE_BLOCK
    block_expert = jnp.minimum(jnp.sum((pend[None, :] <= block_first[:, None]).astype(jnp.int32), axis=1),
                               N_EXPERTS - 1)
    dest_tiles = jnp.transpose(dest.reshape(n // TM, TM, 2), (0, 2, 1)).reshape(n // TM, 2 * TM)
    return block_expert, slot_tok, dest_tiles


def _trunk(x, p, nseq, seq):
    n = nseq * seq
    depth = p['w_in'].shape[0]
    rope = _rope_tables(seq)
    fnet_consts = _fnet_constants(seq)
    tri = jnp.asarray(np.tril(np.ones((TM, TM), np.float32), -1), BF16)
    final_g = p['final_norm_g']
    layers = {k: v for k, v in p.items() if k != 'final_norm_g'}

    def layer(x, w):
        u = _in_proj(x, w['norm1_g'], w['w_in'].astype(BF16), rope, seq)
        ya = _conv_mixer(u, w['conv_w'], w['conv_b'], w['conv_norm_g'], w['conv_norm_b'], nseq, seq)
        yb = _swa_mixer(u, w['attn_sink'], nseq, seq)
        yc = _fnet_mixer(u, fnet_consts, nseq, seq)
        yd = _nat_mixer(u, w['nat_rpb'], nseq, seq)
        wr3, br = _router_weights(w['router_group_w'], w['router_group_b'],
                                  w['router_expert_w'], w['router_expert_b'])
        x2, h2, route, counts = _out_proj((ya, yb, yc, yd), x, w['mix_norm_g'], w['w_out'].astype(BF16),
                                          w['norm2_g'], wr3, br, tri)
        block_expert, slot_tok, dest_tiles = _routing_tables(route, counts, n)
        out_sorted = _moe_blocks(block_expert, slot_tok, h2, w['expert_w_gate'].astype(BF16),
                                 w['expert_w_up'].astype(BF16), w['expert_w_down'].astype(BF16))
        return x2, route, dest_tiles, out_sorted

    def body(x, w):
        x2, route, dest_tiles, out_sorted = layer(x, w)
        return _combine(dest_tiles, x2, route, out_sorted, final_g, final_norm=False), None

    head = {k: v[:depth - 1] for k, v in layers.items()}
    last = {k: v[depth - 1] for k, v in layers.items()}
    if depth > 1:
        x, _ = lax.scan(body, x, head)
    x2, route, dest_tiles, out_sorted = layer(x, last)
    return _combine(dest_tiles, x2, route, out_sorted, final_g, final_norm=True)


@jax.jit
def _forward(x_prompt, x_sample, p):
    bp, seq, d = x_prompt.shape
    bs = x_sample.shape[0]
    assert x_sample.shape[1] == seq and d == D_MODEL
    x = jnp.concatenate([x_prompt.reshape(bp * seq, d), x_sample.reshape(bs * seq, d)], 0)
    y = _trunk(x, p, bp + bs, seq)
    return (y[:bp * seq].reshape(bp, seq, d), y[bp * seq:].reshape(bs, seq, d))


def kernel(x_prompt, x_sample, norm1_g, w_in, conv_w, conv_b, conv_norm_g, conv_norm_b, attn_sink, nat_rpb,
           mix_norm_g, w_out, norm2_g, router_group_w, router_group_b, router_expert_w, router_expert_b,
           expert_w_gate, expert_w_up, expert_w_down, final_norm_g):
    p = dict(norm1_g=norm1_g, w_in=w_in, conv_w=conv_w, conv_b=conv_b, conv_norm_g=conv_norm_g,
             conv_norm_b=conv_norm_b, attn_sink=attn_sink, nat_rpb=nat_rpb, mix_norm_g=mix_norm_g, w_out=w_out,
             norm2_g=norm2_g, router_group_w=router_group_w, router_group_b=router_group_b,
             router_expert_w=router_expert_w, router_expert_b=router_expert_b, expert_w_gate=expert_w_gate,
             expert_w_up=expert_w_up, expert_w_down=expert_w_down, final_norm_g=final_norm_g)
    return _forward(x_prompt, x_sample, p)
```

```python
import functools
import math

import numpy as np
import jax
import jax.numpy as jnp
from jax import lax
from jax.experimental import pallas as pl
from jax.experimental.pallas import tpu as pltpu
from jax.experimental.pallas import tpu_sc as plsc

F32 = jnp.float32
BF16 = jnp.bfloat16

D_MODEL = 1024
GROUP_WIDTH = 256
HEAD_DIM = 64
CONV_WIDTH = 31
CONV_HALO = 16
SWA_HEADS = 4
SWA_KV_HEADS = 2
WINDOW = 128
ROPE_THETA = 500000.0
ROPE_DIM = 16
FNET_CH = 64
FNET_S2 = 256
NAT_HEADS = 4
GRID_W = 64
NAT_KH = 8
NAT_KW = 16
N_EXPERT_GROUPS = 4
EXPERTS_PER_GROUP = 8
N_EXPERTS = 32
D_EXPERT = 512
EPS = 1e-6
IN_WIDTH = 2048
NEG = -1e30

COL_CONV = 0
COL_SWA_Q = 512
COL_SWA_KV = 768
COL_FNET = 1024
COL_NAT_Q = 1280
COL_NAT_K = 1536
COL_NAT_V = 1792

TM = 512
MOE_BLOCK = 256
VMEM_LIMIT = 56 * 1024 * 1024


def _cparams(*sem):
    return pltpu.CompilerParams(dimension_semantics=sem, vmem_limit_bytes=VMEM_LIMIT)


SC_CHUNK = 32


def _sc_gather(table, idx):
    n_rows, width = idx.shape[0], table.shape[1]
    info = plsc.get_sparse_core_info()
    n_cores, n_workers = info.num_cores, info.num_cores * info.num_subcores
    per_worker = n_rows // n_workers
    n_chunks = per_worker // SC_CHUNK
    assert per_worker * n_workers == n_rows and n_chunks * SC_CHUNK == per_worker and n_chunks % 2 == 0
    mesh = plsc.VectorSubcoreMesh(core_axis_name="c", subcore_axis_name="s")
    buf = lambda: pltpu.VMEM((SC_CHUNK, width), table.dtype)
    ids = lambda: pltpu.VMEM((SC_CHUNK,), jnp.int32)

    @functools.partial(
        pl.kernel, mesh=mesh, out_type=jax.ShapeDtypeStruct((n_rows, width), table.dtype),
        scratch_types=[ids(), ids(), buf(), buf(), pltpu.SemaphoreType.DMA, pltpu.SemaphoreType.DMA])
    def gather_kernel(table_hbm, idx_hbm, out_hbm, idx0, idx1, rows0, rows1, sem0, sem1):
        base = (lax.axis_index("s") * n_cores + lax.axis_index("c")) * per_worker

        def fetch(chunk, idx_v, rows_v, sem):
            off = pl.multiple_of(base + chunk * SC_CHUNK, SC_CHUNK)
            pltpu.sync_copy(idx_hbm.at[pl.ds(off, SC_CHUNK)], idx_v)
            pltpu.async_copy(table_hbm.at[idx_v], rows_v, sem)

        def flush(chunk, idx_v, rows_v, sem):
            off = pl.multiple_of(base + chunk * SC_CHUNK, SC_CHUNK)
            pltpu.make_async_copy(table_hbm.at[idx_v], rows_v, sem).wait()
            pltpu.sync_copy(rows_v, out_hbm.at[pl.ds(off, SC_CHUNK)])

        fetch(0, idx0, rows0, sem0)

        @pl.loop(0, n_chunks, step=2)
        def _(j):
            fetch(j + 1, idx1, rows1, sem1)
            flush(j, idx0, rows0, sem0)

            @pl.when(j + 2 < n_chunks)
            def _():
                fetch(j + 2, idx0, rows0, sem0)

            flush(j + 1, idx1, rows1, sem1)

    return gather_kernel(table, idx)


def _split_dot(a, m_bf16):
    hi = a.astype(BF16)
    lo = (a - hi.astype(F32)).astype(BF16)
    return (jnp.dot(hi, m_bf16, preferred_element_type=F32)
            + jnp.dot(lo, m_bf16, preferred_element_type=F32))


def _in_proj_kernel(x_ref, g_ref, w_ref, cos_ref, sina_ref, sinb_ref, u_ref):
    x = x_ref[...]
    h = x * lax.rsqrt(jnp.mean(x * x, axis=-1, keepdims=True) + EPS) * g_ref[...]
    u = jnp.dot(h.astype(BF16), w_ref[...], preferred_element_type=F32)
    u_ref[:, :COL_SWA_Q] = u[:, :COL_SWA_Q].astype(BF16)
    cosf, sina, sinb = cos_ref[...], sina_ref[...], sinb_ref[...]
    for c in range(COL_SWA_Q, COL_SWA_KV + 128, 128):
        t = u[:, c:c + 128]
        fwd = pltpu.roll(t, 128 - ROPE_DIM // 2, axis=1)
        bwd = pltpu.roll(t, ROPE_DIM // 2, axis=1)
        u_ref[:, c:c + 128] = (t * cosf + fwd * sina + bwd * sinb).astype(BF16)
    u_ref[:, COL_SWA_KV + 128:] = u[:, COL_SWA_KV + 128:].astype(BF16)


def _in_proj(x, g, w_bf16, rope, seq):
    n = x.shape[0]
    tiles_per_seq = seq // TM
    cosf, sina, sinb = rope
    rope_spec = pl.BlockSpec((TM, 128), lambda i: (i % tiles_per_seq, 0))
    return pl.pallas_call(
        _in_proj_kernel,
        grid=(n // TM,),
        in_specs=[pl.BlockSpec((TM, D_MODEL), lambda i: (i, 0)),
                  pl.BlockSpec((1, D_MODEL), lambda i: (0, 0)),
                  pl.BlockSpec((D_MODEL, IN_WIDTH), lambda i: (0, 0)),
                  rope_spec, rope_spec, rope_spec],
        out_specs=pl.BlockSpec((TM, IN_WIDTH), lambda i: (i, 0)),
        out_shape=jax.ShapeDtypeStruct((n, IN_WIDTH), BF16),
        compiler_params=_cparams("parallel"),
        name="in_proj",
    )(x, g.reshape(1, D_MODEL), w_bf16, cosf, sina, sinb)


def _rope_tables(seq):
    half = ROPE_DIM // 2
    inv = jnp.float32(ROPE_THETA) ** (-jnp.arange(half, dtype=F32) * 2.0 / ROPE_DIM)
    ang = jnp.arange(seq).astype(F32)[:, None] * inv[None, :]
    cos, sin = jnp.cos(ang), jnp.sin(ang)
    ones = jnp.ones((seq, HEAD_DIM - ROPE_DIM), F32)
    zeros8 = jnp.zeros((seq, half), F32)
    zeros48 = jnp.zeros((seq, HEAD_DIM - ROPE_DIM), F32)
    cosf = jnp.concatenate([cos, cos, ones], -1)
    sina = jnp.concatenate([-sin, zeros8, zeros48], -1)
    sinb = jnp.concatenate([zeros8, sin, zeros48], -1)
    return tuple(jnp.tile(t, (1, 128 // HEAD_DIM)) for t in (cosf, sina, sinb))


CONV_CHUNK = 128


def _conv_kernel(u_ref, w_ref, b_ref, ng_ref, nb_ref, avg_ref, o_ref, vpad_ref, *, seq):
    zeros = jnp.zeros((CONV_HALO, GROUP_WIDTH), F32)
    vpad_ref[0:CONV_HALO, :] = zeros
    vpad_ref[CONV_HALO + seq:CONV_HALO + seq + CONV_HALO, :] = zeros

    def glu(i, carry):
        r = pl.multiple_of(i * CONV_CHUNK, CONV_CHUNK)
        blk = u_ref[pl.ds(r, CONV_CHUNK), :].astype(F32)
        a, g = blk[:, :GROUP_WIDTH], blk[:, GROUP_WIDTH:]
        vpad_ref[pl.ds(CONV_HALO + r, CONV_CHUNK), :] = a / (1.0 + jnp.exp(-g))
        return carry

    lax.fori_loop(0, seq // CONV_CHUNK, glu, 0)
    avg = avg_ref[...]
    bias, ng, nb = b_ref[...], ng_ref[...], nb_ref[...]

    def conv(i, carry):
        r = pl.multiple_of(i * CONV_CHUNK, CONV_CHUNK)
        acc = jnp.zeros((CONV_CHUNK, GROUP_WIDTH), F32) + bias
        win_rows = CONV_CHUNK + 2 * CONV_HALO
        win = vpad_ref[pl.ds(r, win_rows), :]
        for b in range(8):
            shifted = win if b == 0 else pltpu.roll(win, win_rows - b, axis=0)
            for a in range(4):
                j = 8 * a + b - (CONV_HALO - CONV_WIDTH // 2)
                if 0 <= j < CONV_WIDTH:
                    acc = acc + w_ref[j:j + 1, :] * shifted[8 * a:8 * a + CONV_CHUNK, :]
        mu = _split_dot(acc, avg)
        d = acc - mu
        var = _split_dot(d * d, avg)
        y = d * lax.rsqrt(var + EPS) * ng + nb
        o_ref[pl.ds(r, CONV_CHUNK), :] = (y / (1.0 + jnp.exp(-y))).astype(BF16)
        return carry

    lax.fori_loop(0, seq // CONV_CHUNK, conv, 0)


def _conv_mixer(u, conv_w, conv_b, norm_g, norm_b, nseq, seq):
    gid = np.arange(GROUP_WIDTH) // HEAD_DIM
    avg = jnp.asarray((gid[:, None] == gid[None, :]).astype(np.float32) / HEAD_DIM, BF16)
    row = lambda v: v.reshape(1, GROUP_WIDTH)
    const = lambda shape: pl.BlockSpec(shape, lambda b: (0, 0))
    return pl.pallas_call(
        functools.partial(_conv_kernel, seq=seq),
        grid=(nseq,),
        in_specs=[pl.BlockSpec((seq, 2 * GROUP_WIDTH), lambda b: (b, COL_CONV // (2 * GROUP_WIDTH))),
                  const((CONV_WIDTH, GROUP_WIDTH)), const((1, GROUP_WIDTH)), const((1, GROUP_WIDTH)),
                  const((1, GROUP_WIDTH)), const((GROUP_WIDTH, GROUP_WIDTH))],
        out_specs=pl.BlockSpec((seq, GROUP_WIDTH), lambda b: (b, 0)),
        out_shape=jax.ShapeDtypeStruct((nseq * seq, GROUP_WIDTH), BF16),
        scratch_shapes=[pltpu.VMEM((seq + 2 * CONV_HALO, GROUP_WIDTH), F32)],
        compiler_params=_cparams("parallel"),
        name="conv_mixer",
    )(u, conv_w, row(conv_b), row(norm_g), row(norm_b), avg)


SWA_TQ = 256


def _swa_kernel(sink_ref, q_ref, kv_ref, o_ref, *, seq):
    i = pl.program_id(1)
    kw = SWA_TQ + 2 * WINDOW
    kstart = pl.multiple_of(jnp.clip(i * SWA_TQ - WINDOW, 0, seq - kw), WINDOW)
    kv = kv_ref[pl.ds(kstart, kw), :]
    q = q_ref[...]
    qpos = i * SWA_TQ + lax.broadcasted_iota(jnp.int32, (SWA_TQ, kw), 0)
    kpos = kstart + lax.broadcasted_iota(jnp.int32, (SWA_TQ, kw), 1)
    valid = jnp.abs(qpos - kpos) <= WINDOW
    rep = SWA_HEADS // SWA_KV_HEADS
    for g in range(SWA_KV_HEADS):
        k = kv[:, g * HEAD_DIM:(g + 1) * HEAD_DIM]
        v = kv[:, SWA_KV_HEADS * HEAD_DIM + g * HEAD_DIM:SWA_KV_HEADS * HEAD_DIM + (g + 1) * HEAD_DIM]
        for r in range(rep):
            h = g * rep + r
            qh = q[:, h * HEAD_DIM:(h + 1) * HEAD_DIM]
            s = lax.dot_general(qh, k, (((1,), (1,)), ((), ())), preferred_element_type=F32)
            s = jnp.where(valid, s * (HEAD_DIM ** -0.5), NEG)
            sink = sink_ref[h]
            m = jnp.maximum(jnp.max(s, axis=-1, keepdims=True), sink)
            p = jnp.exp(s - m)
            denom = jnp.sum(p, axis=-1, keepdims=True) + jnp.exp(sink - m)
            o = jnp.dot(p.astype(BF16), v, preferred_element_type=F32) / denom
            o_ref[:, h * HEAD_DIM:(h + 1) * HEAD_DIM] = o.astype(BF16)


def _swa_mixer(u, sink, nseq, seq):
    nq = seq // SWA_TQ
    return pl.pallas_call(
        functools.partial(_swa_kernel, seq=seq),
        grid=(nseq, nq),
        in_specs=[pl.BlockSpec(memory_space=pltpu.SMEM),
                  pl.BlockSpec((SWA_TQ, GROUP_WIDTH), lambda b, i: (b * nq + i, COL_SWA_Q // GROUP_WIDTH)),
                  pl.BlockSpec((seq, GROUP_WIDTH), lambda b, i: (b, COL_SWA_KV // GROUP_WIDTH))],
        out_specs=pl.BlockSpec((SWA_TQ, GROUP_WIDTH), lambda b, i: (b * nq + i, 0)),
        out_shape=jax.ShapeDtypeStruct((nseq * seq, GROUP_WIDTH), BF16),
        compiler_params=_cparams("parallel", "arbitrary"),
        name="swa_mixer",
    )(sink, u, u)


def _fnet_kernel(x_ref, m_ref, cc_ref, sc_ref, o_ref, z_ref, *, seq):
    r1 = seq // FNET_S2
    half = r1 // 2
    xs = [x_ref[s1 * FNET_S2:(s1 + 1) * FNET_S2, :].astype(F32) for s1 in range(r1)]
    sym = {s1: xs[s1] + xs[r1 - s1] for s1 in range(1, half)}
    asym = {s1: xs[s1] - xs[r1 - s1] for s1 in range(1, half)}

    def weighted(terms):
        acc = None
        for w, t in terms:
            if abs(w) < 1e-9:
                continue
            if abs(w - 1.0) < 1e-9:
                acc = t if acc is None else acc + t
            elif abs(w + 1.0) < 1e-9:
                acc = -t if acc is None else acc - t
            else:
                acc = w * t if acc is None else acc + w * t
        return jnp.zeros((FNET_S2, GROUP_WIDTH), F32) if acc is None else acc

    for k1 in range(half + 1):
        re_terms = [(1.0, xs[0]), (float((-1) ** k1), xs[half])]
        re_terms += [(math.cos(2 * math.pi * k1 * s1 / r1), sym[s1]) for s1 in range(1, half)]
        im_terms = [(-math.sin(2 * math.pi * k1 * s1 / r1), asym[s1]) for s1 in range(1, half)]
        zr = weighted(re_terms).astype(BF16)
        zi = weighted(im_terms)
        z_ref[k1, :FNET_S2, :] = zr
        z_ref[k1, FNET_S2:, :] = zi.astype(BF16)
        if 0 < k1 < half:
            z_ref[r1 - k1, :FNET_S2, :] = zr
            z_ref[r1 - k1, FNET_S2:, :] = (-zi).astype(BF16)

    scale = 1.0 / math.sqrt(seq * FNET_CH)
    cc, sc = cc_ref[...], sc_ref[...]
    for k1 in range(r1):
        y = jnp.dot(m_ref[k1], z_ref[k1], preferred_element_type=F32)
        out = (jnp.dot(y[:FNET_S2].astype(BF16), cc, preferred_element_type=F32)
               + jnp.dot(y[FNET_S2:].astype(BF16), sc, preferred_element_type=F32))
        o_ref[:, k1 * GROUP_WIDTH:(k1 + 1) * GROUP_WIDTH] = (out * scale).astype(BF16)


def _fnet_constants(seq):
    r1 = seq // FNET_S2
    k = (np.arange(r1)[:, None, None] + r1 * np.arange(FNET_S2)[None, :, None]).astype(np.float64)
    s2 = np.arange(FNET_S2)[None, None, :].astype(np.float64)
    ang = 2 * np.pi * ((k * s2) % seq) / seq
    c, s = np.cos(ang), np.sin(ang)
    m = np.concatenate([np.concatenate([c, s], -1), np.concatenate([-s, c], -1)], 1)
    ch = np.arange(GROUP_WIDTH)
    same = (ch[:, None] // FNET_CH) == (ch[None, :] // FNET_CH)
    ang_c = 2 * np.pi * ((ch[:, None] % FNET_CH) * (ch[None, :] % FNET_CH) % FNET_CH) / FNET_CH
    cc = np.where(same, np.cos(ang_c), 0.0)
    sc = np.where(same, np.sin(ang_c), 0.0)
    return (jnp.asarray(m, BF16), jnp.asarray(cc, BF16), jnp.asarray(sc, BF16))


def _fnet_mixer(u, consts, nseq, seq):
    r1 = seq // FNET_S2
    m, cc, sc = consts
    out = pl.pallas_call(
        functools.partial(_fnet_kernel, seq=seq),
        grid=(nseq,),
        in_specs=[pl.BlockSpec((seq, GROUP_WIDTH), lambda b: (b, COL_FNET // GROUP_WIDTH)),
                  pl.BlockSpec((r1, 2 * FNET_S2, 2 * FNET_S2), lambda b: (0, 0, 0)),
                  pl.BlockSpec((GROUP_WIDTH, GROUP_WIDTH), lambda b: (0, 0)),
                  pl.BlockSpec((GROUP_WIDTH, GROUP_WIDTH), lambda b: (0, 0))],
        out_specs=pl.BlockSpec((FNET_S2, r1 * GROUP_WIDTH), lambda b: (b, 0)),
        out_shape=jax.ShapeDtypeStruct((nseq * FNET_S2, r1 * GROUP_WIDTH), BF16),
        scratch_shapes=[pltpu.VMEM((r1, 2 * FNET_S2, GROUP_WIDTH), BF16)],
        compiler_params=_cparams("parallel"),
        name="fnet_mixer",
    )(u, m, cc, sc)
    return out.reshape(nseq * seq, GROUP_WIDTH)


NAT_G = 4


def _nat_geometry(rows, kh):
    wr = NAT_G + kh
    n_groups = rows // NAT_G
    w0 = np.clip(NAT_G * np.arange(n_groups) - kh // 2, 0, rows - wr)
    patterns, pat_id = [], []
    for g in range(n_groups):
        r = NAT_G * g + np.arange(NAT_G)
        kr0 = np.clip(r - kh // 2, 0, rows - kh)
        kr = w0[g] + np.arange(wr)
        valid = (kr[None, :] >= kr0[:, None]) & (kr[None, :] < kr0[:, None] + kh)
        dr = np.where(valid, kr[None, :] - r[:, None] + NAT_KH - 1, 0)
        assert valid.sum(1).min() == kh and dr.min() >= 0 and dr.max() <= 2 * NAT_KH - 2
        for i, (v, d) in enumerate(patterns):
            if np.array_equal(v, valid) and np.array_equal(d, dr):
                pat_id.append(i)
                break
        else:
            pat_id.append(len(patterns))
            patterns.append((valid, dr))
    return w0.astype(np.int32), np.asarray(pat_id, np.int32), patterns


def _nat_bias_kernel(rpb_ref, onehot_ref, o_ref):
    a = rpb_ref[...]
    a1 = a.astype(BF16)
    r1 = a - a1.astype(F32)
    a2 = r1.astype(BF16)
    a3 = (r1 - a2.astype(F32)).astype(BF16)
    e = onehot_ref[...]
    o_ref[...] = (jnp.dot(a1, e, preferred_element_type=F32) + jnp.dot(a2, e, preferred_element_type=F32)
                  + jnp.dot(a3, e, preferred_element_type=F32))


def _nat_bias_table(rpb, patterns):
    n_dr, n_dc = 2 * NAT_KH - 1, 2 * NAT_KW - 1
    c = np.arange(GRID_W)
    dc = np.clip(c[None, :] - c[:, None] + (NAT_KW - 1), 0, n_dc - 1)
    onehot = (np.arange(n_dc)[:, None, None] == dc[None]).astype(np.float32)
    onehot = np.concatenate([onehot.reshape(n_dc, -1), np.zeros((32 - n_dc, GRID_W * GRID_W), np.float32)], 0)
    rows = NAT_HEADS * n_dr
    rows_pad = -(-rows // 8) * 8
    rpb2 = jnp.zeros((rows_pad, 32), F32).at[:rows, :n_dc].set(rpb.reshape(rows, n_dc))
    t = pl.pallas_call(
        _nat_bias_kernel,
        out_shape=jax.ShapeDtypeStruct((rows_pad, GRID_W * GRID_W), F32),
        name="nat_bias",
    )(rpb2, jnp.asarray(onehot, BF16))
    t = t[:rows].reshape(NAT_HEADS, n_dr, GRID_W, GRID_W)
    c0 = np.clip(c - NAT_KW // 2, 0, GRID_W - NAT_KW)
    col_ok = (c[None, :] >= c0[:, None]) & (c[None, :] < c0[:, None] + NAT_KW)
    mask = jnp.asarray(np.where(col_ok, 0.0, NEG), F32)
    tabs = []
    for valid, dr in patterns:
        wr = valid.shape[1]
        win = t[:, dr] + mask[None, None, None]
        win = jnp.where(jnp.asarray(valid)[None, :, :, None, None], win, NEG)
        win = jnp.transpose(win, (0, 1, 3, 2, 4))
        tabs.append(win.reshape(NAT_HEADS, NAT_G * GRID_W, wr * GRID_W))
    return jnp.stack(tabs, 0)


def _nat_kernel(w0_ref, pat_ref, q_ref, k_ref, v_ref, bias_ref, o_ref, *, wk):
    g = pl.program_id(1)
    kbase = pl.multiple_of(w0_ref[g] * GRID_W, GRID_W)
    kk = k_ref[pl.ds(kbase, wk), :]
    vv = v_ref[pl.ds(kbase, wk), :]
    q = q_ref[...]
    tq = q.shape[0]
    lane = lax.broadcasted_iota(jnp.int32, (tq, 128), 1)
    upper, lower = lane >= HEAD_DIM, lane < HEAD_DIM
    for hp in range(NAT_HEADS // 2):
        sl = slice(hp * 128, (hp + 1) * 128)
        q2, k2, v2 = q[:, sl], kk[:, sl], vv[:, sl]
        outs = []
        for r in range(2):
            qm = jnp.where(upper if r else lower, q2, jnp.zeros_like(q2))
            s = lax.dot_general(qm, k2, (((1,), (1,)), ((), ())), preferred_element_type=F32)
            s = s * (HEAD_DIM ** -0.5) + bias_ref[0, 2 * hp + r]
            m = jnp.max(s, axis=-1, keepdims=True)
            p = jnp.exp(s - m)
            denom = jnp.sum(p, axis=-1, keepdims=True)
            outs.append(jnp.dot(p.astype(BF16), v2, preferred_element_type=F32) / denom)
        o_ref[:, sl] = jnp.where(upper, outs[1], outs[0]).astype(BF16)


def _nat_mixer(u, rpb, nseq, seq):
    rows = seq // GRID_W
    kh = min(NAT_KH, rows)
    w0, pat_id, patterns = _nat_geometry(rows, kh)
    bias_tab = _nat_bias_table(rpb, patterns)
    n_groups = rows // NAT_G
    tq = NAT_G * GRID_W
    wk = (NAT_G + kh) * GRID_W
    kv_spec = lambda col: pl.BlockSpec((seq, GROUP_WIDTH), lambda b, g, w0, pat: (b, col // GROUP_WIDTH))
    grid_spec = pltpu.PrefetchScalarGridSpec(
        num_scalar_prefetch=2,
        grid=(nseq, n_groups),
        in_specs=[pl.BlockSpec((tq, GROUP_WIDTH),
                               lambda b, g, w0, pat: (b * n_groups + g, COL_NAT_Q // GROUP_WIDTH)),
                  kv_spec(COL_NAT_K), kv_spec(COL_NAT_V),
                  pl.BlockSpec((1, NAT_HEADS, tq, wk), lambda b, g, w0, pat: (pat[g], 0, 0, 0))],
        out_specs=pl.BlockSpec((tq, GROUP_WIDTH), lambda b, g, w0, pat: (b * n_groups + g, 0)),
    )
    return pl.pallas_call(
        functools.partial(_nat_kernel, wk=wk),
        grid_spec=grid_spec,
        out_shape=jax.ShapeDtypeStruct((nseq * seq, GROUP_WIDTH), BF16),
        compiler_params=_cparams("parallel", "arbitrary"),
        name="nat_mixer",
    )(jnp.asarray(w0), jnp.asarray(pat_id), u, u, u, bias_tab)


ROUTE_W = 128


def _out_proj_kernel(ya_ref, yb_ref, yc_ref, yd_ref, x_ref, mg_ref, wo_ref, g2_ref, wr_ref, br_ref, tri_ref,
                     x2_ref, h2_ref, route_ref, count_ref, base_ref):
    @pl.when(pl.program_id(0) == 0)
    def _():
        base_ref[...] = jnp.zeros_like(base_ref)

    acc = x_ref[...]
    for j, y_ref in enumerate((ya_ref, yb_ref, yc_ref, yd_ref)):
        y = y_ref[...].astype(F32)
        y = y * lax.rsqrt(jnp.mean(y * y, axis=-1, keepdims=True) + EPS)
        y = y * mg_ref[:, j * GROUP_WIDTH:(j + 1) * GROUP_WIDTH]
        acc = acc + jnp.dot(y.astype(BF16), wo_ref[j * GROUP_WIDTH:(j + 1) * GROUP_WIDTH, :],
                            preferred_element_type=F32)
    x2_ref[...] = acc
    h2 = acc * lax.rsqrt(jnp.mean(acc * acc, axis=-1, keepdims=True) + EPS) * g2_ref[...]
    h2_ref[...] = h2

    a1 = h2.astype(BF16)
    rem = h2 - a1.astype(F32)
    a2 = rem.astype(BF16)
    a3 = (rem - a2.astype(F32)).astype(BF16)
    w1, w2, w3 = wr_ref[0], wr_ref[1], wr_ref[2]
    dot = lambda a, w: jnp.dot(a, w, preferred_element_type=F32)
    logits = (dot(a1, w1) + (dot(a1, w2) + dot(a2, w1))
              + (dot(a2, w2) + dot(a1, w3) + dot(a3, w1))) + br_ref[...]

    lane = lax.broadcasted_iota(jnp.int32, logits.shape, 1).astype(F32)
    big = jnp.float32(ROUTE_W)
    is_g = lane < N_EXPERT_GROUPS
    gl = jnp.where(is_g, logits, NEG)
    gm = jnp.max(gl, axis=-1, keepdims=True)
    grp = jnp.min(jnp.where(gl == gm, lane, big), axis=-1, keepdims=True)
    gsum = jnp.sum(jnp.where(is_g, jnp.exp(gl - gm), 0.0), axis=-1, keepdims=True)
    p_grp = 1.0 / gsum
    lo = N_EXPERT_GROUPS + EXPERTS_PER_GROUP * grp
    emask = (lane >= lo) & (lane < lo + EXPERTS_PER_GROUP)
    el = jnp.where(emask, logits, NEG)
    l1 = jnp.max(el, axis=-1, keepdims=True)
    i1 = jnp.min(jnp.where(emask & (el == l1), lane, big), axis=-1, keepdims=True)
    rest = emask & (lane != i1)
    el2 = jnp.where(rest, logits, NEG)
    l2 = jnp.max(el2, axis=-1, keepdims=True)
    i2 = jnp.min(jnp.where(rest & (el2 == l2), lane, big), axis=-1, keepdims=True)
    e21 = jnp.exp(l2 - l1)
    gate0 = p_grp / (1.0 + e21)
    gate1 = p_grp * e21 / (1.0 + e21)
    e0 = i1 - N_EXPERT_GROUPS
    e1 = i2 - N_EXPERT_GROUPS

    hit0 = lane == e0
    hit1 = lane == e1
    onehot = jnp.where(hit0 | hit1, 1.0, 0.0)
    before = jnp.dot(tri_ref[...], onehot.astype(BF16), preferred_element_type=F32) + base_ref[0:1, :]
    rank0 = jnp.sum(jnp.where(hit0, before, 0.0), axis=-1, keepdims=True)
    rank1 = jnp.sum(jnp.where(hit1, before, 0.0), axis=-1, keepdims=True)
    new_base = base_ref[0:1, :] + jnp.sum(onehot, axis=0, keepdims=True)
    base_ref[...] = jnp.broadcast_to(new_base, base_ref.shape)
    count_ref[...] = jnp.broadcast_to(new_base, count_ref.shape)

    rec = jnp.zeros_like(logits)
    for col, val in enumerate((e0, e1, gate0, gate1, rank0, rank1)):
        rec = jnp.where(lane == col, val, rec)
    route_ref[...] = rec


def _out_proj(ys, x, mix_g, wo_bf16, g2, wr3, br, tri):
    n = x.shape[0]
    tile = lambda w: pl.BlockSpec((TM, w), lambda i: (i, 0))
    const = lambda *shape: pl.BlockSpec(shape, lambda i: (0,) * len(shape))
    return pl.pallas_call(
        _out_proj_kernel,
        grid=(n // TM,),
        in_specs=[tile(GROUP_WIDTH)] * 4 + [tile(D_MODEL), const(1, D_MODEL), const(D_MODEL, D_MODEL),
                                           const(1, D_MODEL), const(3, D_MODEL, ROUTE_W), const(1, ROUTE_W),
                                           const(TM, TM)],
        out_specs=[tile(D_MODEL), tile(D_MODEL), tile(ROUTE_W), const(8, ROUTE_W)],
        out_shape=[jax.ShapeDtypeStruct((n, D_MODEL), F32), jax.ShapeDtypeStruct((n, D_MODEL), F32),
                   jax.ShapeDtypeStruct((n, ROUTE_W), F32), jax.ShapeDtypeStruct((8, ROUTE_W), F32)],
        scratch_shapes=[pltpu.VMEM((8, ROUTE_W), F32)],
        compiler_params=_cparams("arbitrary"),
        name="out_proj_router",
    )(*ys, x, mix_g.reshape(1, D_MODEL), wo_bf16, g2.reshape(1, D_MODEL), wr3, br, tri)


def _router_weights(wg, bg, we, be):
    w = jnp.concatenate([wg, we, jnp.zeros((D_MODEL, ROUTE_W - N_EXPERT_GROUPS - N_EXPERTS), F32)], -1)
    w1 = w.astype(BF16)
    r = w - w1.astype(F32)
    w2 = r.astype(BF16)
    w3 = (r - w2.astype(F32)).astype(BF16)
    b = jnp.concatenate([bg, be, jnp.zeros((ROUTE_W - N_EXPERT_GROUPS - N_EXPERTS,), F32)]).reshape(1, ROUTE_W)
    return jnp.stack([w1, w2, w3], 0), b


def _moe_kernel(be_ref, x_ref, wg_ref, wu_ref, wd_ref, o_ref):
    x = x_ref[...].astype(BF16)
    gate = jnp.dot(x, wg_ref[0], preferred_element_type=F32)
    up = jnp.dot(x, wu_ref[0], preferred_element_type=F32)
    hid = (gate / (1.0 + jnp.exp(-gate))) * up
    o_ref[...] = jnp.dot(hid.astype(BF16), wd_ref[0], preferred_element_type=F32)


def _moe_blocks(block_expert, x_sorted, wg, wu, wd):
    nb = block_expert.shape[0]
    wspec = lambda shape: pl.BlockSpec((1,) + shape, lambda b, be: (be[b], 0, 0))
    grid_spec = pltpu.PrefetchScalarGridSpec(
        num_scalar_prefetch=1,
        grid=(nb,),
        in_specs=[pl.BlockSpec((MOE_BLOCK, D_MODEL), lambda b, be: (b, 0)),
                  wspec((D_MODEL, D_EXPERT)), wspec((D_MODEL, D_EXPERT)), wspec((D_EXPERT, D_MODEL))],
        out_specs=pl.BlockSpec((MOE_BLOCK, D_MODEL), lambda b, be: (b, 0)),
    )
    return pl.pallas_call(
        _moe_kernel,
        grid_spec=grid_spec,
        out_shape=jax.ShapeDtypeStruct((nb * MOE_BLOCK, D_MODEL), F32),
        compiler_params=_cparams("arbitrary"),
        name="moe_blocks",
    )(block_expert, x_sorted, wg, wu, wd)


def _combine_kernel(x_ref, route_ref, g_ref, y0_ref, y1_ref, o_ref, *, final_norm):
    route = route_ref[...]
    x = x_ref[...] + route[:, 2:3] * y0_ref[...] + route[:, 3:4] * y1_ref[...]
    if final_norm:
        x = x * lax.rsqrt(jnp.mean(x * x, axis=-1, keepdims=True) + EPS) * g_ref[...]
    o_ref[...] = x


def _combine(x2, route, y_rows, final_g, final_norm):
    n = x2.shape[0]
    n_tiles = n // TM
    return pl.pallas_call(
        functools.partial(_combine_kernel, final_norm=final_norm),
        grid=(n_tiles,),
        in_specs=[pl.BlockSpec((TM, D_MODEL), lambda i: (i, 0)),
                  pl.BlockSpec((TM, ROUTE_W), lambda i: (i, 0)),
                  pl.BlockSpec((1, D_MODEL), lambda i: (0, 0)),
                  pl.BlockSpec((TM, D_MODEL), lambda i: (i, 0)),
                  pl.BlockSpec((TM, D_MODEL), lambda i: (i + n_tiles, 0))],
        out_specs=pl.BlockSpec((TM, D_MODEL), lambda i: (i, 0)),
        out_shape=jax.ShapeDtypeStruct((n, D_MODEL), F32),
        compiler_params=_cparams("parallel"),
        name="moe_combine",
    )(x2, route, final_g.reshape(1, D_MODEL), y_rows, y_rows)


def _routing_tables(route, counts, n):
    e = route[:, 0:2].astype(jnp.int32)
    rank = route[:, 4:6].astype(jnp.int32)
    cnt = counts[0, :N_EXPERTS].astype(jnp.int32)
    padded = (cnt + MOE_BLOCK - 1) // MOE_BLOCK * MOE_BLOCK
    pend = jnp.cumsum(padded)
    pstart = pend - padded
    dest = pstart[e] + rank
    n_blocks = -(-(2 * n) // MOE_BLOCK) + N_EXPERTS
    tok = jnp.broadcast_to(jnp.arange(n, dtype=jnp.int32)[:, None], (n, 2))
    slot_tok = jnp.zeros((n_blocks * MOE_BLOCK,), jnp.int32).at[dest.reshape(-1)].set(tok.reshape(-1))
    block_first = jnp.arange(n_blocks, dtype=jnp.int32) * MOE_BLOCK
    block_expert = jnp.minimum(jnp.sum((pend[None, :] <= block_first[:, None]).astype(jnp.int32), axis=1),
                               N_EXPERTS - 1)
    dest_flat = jnp.transpose(dest, (1, 0)).reshape(2 * n)
    return block_expert, slot_tok, dest_flat


def _trunk(x, p, nseq, seq):
    n = nseq * seq
    depth = p['w_in'].shape[0]
    rope = _rope_tables(seq)
    fnet_consts = _fnet_constants(seq)
    tri = jnp.asarray(np.tril(np.ones((TM, TM), np.float32), -1), BF16)
    final_g = p['final_norm_g']
    layers = {k: v for k, v in p.items() if k != 'final_norm_g'}

    def layer(x, w):
        u = _in_proj(x, w['norm1_g'], w['w_in'].astype(BF16), rope, seq)
        ya = _conv_mixer(u, w['conv_w'], w['conv_b'], w['conv_norm_g'], w['conv_norm_b'], nseq, seq)
        yb = _swa_mixer(u, w['attn_sink'], nseq, seq)
        yc = _fnet_mixer(u, fnet_consts, nseq, seq)
        yd = _nat_mixer(u, w['nat_rpb'], nseq, seq)
        wr3, br = _router_weights(w['router_group_w'], w['router_group_b'],
                                  w['router_expert_w'], w['router_expert_b'])
        x2, h2, route, counts = _out_proj((ya, yb, yc, yd), x, w['mix_norm_g'], w['w_out'].astype(BF16),
                                          w['norm2_g'], wr3, br, tri)
        block_expert, slot_tok, dest_flat = _routing_tables(route, counts, n)
        x_sorted = _sc_gather(h2, slot_tok)
        out_sorted = _moe_blocks(block_expert, x_sorted, w['expert_w_gate'].astype(BF16),
                                 w['expert_w_up'].astype(BF16), w['expert_w_down'].astype(BF16))
        return x2, route, _sc_gather(out_sorted, dest_flat)

    def body(x, w):
        x2, route, y_rows = layer(x, w)
        return _combine(x2, route, y_rows, final_g, final_norm=False), None

    head = {k: v[:depth - 1] for k, v in layers.items()}
    last = {k: v[depth - 1] for k, v in layers.items()}
    if depth > 1:
        x, _ = lax.scan(body, x, head)
    x2, route, y_rows = layer(x, last)
    return _combine(x2, route, y_rows, final_g, final_norm=True)


@jax.jit
def _forward(x_prompt, x_sample, p):
    bp, seq, d = x_prompt.shape
    bs = x_sample.shape[0]
    assert x_sample.shape[1] == seq and d == D_MODEL
    x = jnp.concatenate([x_prompt.reshape(bp * seq, d), x_sample.reshape(bs * seq, d)], 0)
    y = _trunk(x, p, bp + bs, seq)
    return (y[:bp * seq].reshape(bp, seq, d), y[bp * seq:].reshape(bs, seq, d))


def kernel(x_prompt, x_sample, norm1_g, w_in, conv_w, conv_b, conv_norm_g, conv_norm_b, attn_sink, nat_rpb,
           mix_norm_g, w_out, norm2_g, router_group_w, router_group_b, router_expert_w, router_expert_b,
           expert_w_gate, expert_w_up, expert_w_down, final_norm_g):
    p = dict(norm1_g=norm1_g, w_in=w_in, conv_w=conv_w, conv_b=conv_b, conv_norm_g=conv_norm_g,
             conv_norm_b=conv_norm_b, attn_sink=attn_sink, nat_rpb=nat_rpb, mix_norm_g=mix_norm_g, w_out=w_out,
             norm2_g=norm2_g, router_group_w=router_group_w, router_group_b=router_group_b,
             router_expert_w=router_expert_w, router_expert_b=router_expert_b, expert_w_gate=expert_w_gate,
             expert_w_up=expert_w_up, expert_w_down=expert_w_down, final_norm_g=final_norm_g)
    return _forward(x_prompt, x_sample, p)
```

```python
import functools
import math

import numpy as np
import jax
import jax.numpy as jnp
from jax import lax
from jax.experimental import pallas as pl
from jax.experimental.pallas import tpu as pltpu
from jax.experimental.pallas import tpu_sc as plsc

F32 = jnp.float32
BF16 = jnp.bfloat16

D_MODEL = 1024
GROUP_WIDTH = 256
HEAD_DIM = 64
CONV_WIDTH = 31
CONV_HALO = 16
SWA_HEADS = 4
SWA_KV_HEADS = 2
WINDOW = 128
ROPE_THETA = 500000.0
ROPE_DIM = 16
FNET_CH = 64
FNET_S2 = 256
NAT_HEADS = 4
GRID_W = 64
NAT_KH = 8
NAT_KW = 16
N_EXPERT_GROUPS = 4
EXPERTS_PER_GROUP = 8
N_EXPERTS = 32
D_EXPERT = 512
EPS = 1e-6
IN_WIDTH = 2048
NEG = -1e30

COL_CONV = 0
COL_SWA_Q = 512
COL_SWA_KV = 768
COL_FNET = 1024
COL_NAT_Q = 1280
COL_NAT_K = 1536
COL_NAT_V = 1792

TM = 512
MOE_BLOCK = 512
VMEM_LIMIT = 56 * 1024 * 1024


def _cparams(*sem):
    return pltpu.CompilerParams(dimension_semantics=sem, vmem_limit_bytes=VMEM_LIMIT)


SC_CHUNK = 32


def _sc_gather(table, idx):
    n_rows, width = idx.shape[0], table.shape[1]
    info = plsc.get_sparse_core_info()
    n_cores, n_workers = info.num_cores, info.num_cores * info.num_subcores
    per_worker = n_rows // n_workers
    n_chunks = per_worker // SC_CHUNK
    assert per_worker * n_workers == n_rows and n_chunks * SC_CHUNK == per_worker and n_chunks % 2 == 0
    mesh = plsc.VectorSubcoreMesh(core_axis_name="c", subcore_axis_name="s")
    buf = lambda: pltpu.VMEM((SC_CHUNK, width), table.dtype)
    ids = lambda: pltpu.VMEM((SC_CHUNK,), jnp.int32)

    @functools.partial(
        pl.kernel, mesh=mesh, out_type=jax.ShapeDtypeStruct((n_rows, width), table.dtype),
        scratch_types=[ids(), ids(), buf(), buf(), pltpu.SemaphoreType.DMA, pltpu.SemaphoreType.DMA])
    def gather_kernel(table_hbm, idx_hbm, out_hbm, idx0, idx1, rows0, rows1, sem0, sem1):
        base = (lax.axis_index("s") * n_cores + lax.axis_index("c")) * per_worker

        def fetch(chunk, idx_v, rows_v, sem):
            off = pl.multiple_of(base + chunk * SC_CHUNK, SC_CHUNK)
            pltpu.sync_copy(idx_hbm.at[pl.ds(off, SC_CHUNK)], idx_v)
            pltpu.async_copy(table_hbm.at[idx_v], rows_v, sem)

        def flush(chunk, idx_v, rows_v, sem):
            off = pl.multiple_of(base + chunk * SC_CHUNK, SC_CHUNK)
            pltpu.make_async_copy(table_hbm.at[idx_v], rows_v, sem).wait()
            pltpu.sync_copy(rows_v, out_hbm.at[pl.ds(off, SC_CHUNK)])

        fetch(0, idx0, rows0, sem0)

        @pl.loop(0, n_chunks, step=2)
        def _(j):
            fetch(j + 1, idx1, rows1, sem1)
            flush(j, idx0, rows0, sem0)

            @pl.when(j + 2 < n_chunks)
            def _():
                fetch(j + 2, idx0, rows0, sem0)

            flush(j + 1, idx1, rows1, sem1)

    return gather_kernel(table, idx)


def _sc_scatter_pairs(rows, dest, n_out):
    n_rows, width = rows.shape
    info = plsc.get_sparse_core_info()
    n_cores, n_workers = info.num_cores, info.num_cores * info.num_subcores
    per_worker = n_rows // n_workers
    n_chunks = per_worker // SC_CHUNK
    assert per_worker * n_workers == n_rows and n_chunks * SC_CHUNK == per_worker and n_chunks % 2 == 0
    chunks_total = n_rows // SC_CHUNK
    dest = dest.reshape(2 * chunks_total, SC_CHUNK)
    mesh = plsc.VectorSubcoreMesh(core_axis_name="c", subcore_axis_name="s")
    buf = lambda: pltpu.VMEM((SC_CHUNK, width), rows.dtype)
    ids = lambda: pltpu.VMEM((1, SC_CHUNK), jnp.int32)
    dma = pltpu.SemaphoreType.DMA

    @functools.partial(
        pl.kernel, mesh=mesh, out_type=jax.ShapeDtypeStruct((n_out, width), rows.dtype),
        scratch_types=[ids(), ids(), ids(), ids(), buf(), buf(), dma, dma, dma, dma])
    def scatter_kernel(rows_hbm, dest_hbm, out_hbm, ia0, ib0, ia1, ib1, rows0, rows1, ld0, ld1, st0, st1):
        first_chunk = (lax.axis_index("s") * n_cores + lax.axis_index("c")) * n_chunks

        def load(chunk, ia, ib, rows_v, ld):
            c = first_chunk + chunk
            pltpu.sync_copy(dest_hbm.at[pl.ds(c, 1)], ia)
            pltpu.sync_copy(dest_hbm.at[pl.ds(chunks_total + c, 1)], ib)
            pltpu.async_copy(rows_hbm.at[pl.ds(pl.multiple_of(c * SC_CHUNK, SC_CHUNK), SC_CHUNK)], rows_v, ld)

        def scatter(ia, ib, rows_v, ld, st):
            pltpu.make_async_copy(rows_hbm.at[pl.ds(0, SC_CHUNK)], rows_v, ld).wait()
            pltpu.async_copy(rows_v, out_hbm.at[ia.at[0]], st)
            pltpu.async_copy(rows_v, out_hbm.at[ib.at[0]], st)
            pltpu.make_async_copy(rows_v, out_hbm.at[ia.at[0]], st).wait()
            pltpu.make_async_copy(rows_v, out_hbm.at[ib.at[0]], st).wait()

        load(0, ia0, ib0, rows0, ld0)

        @pl.loop(0, n_chunks, step=2)
        def _(j):
            load(j + 1, ia1, ib1, rows1, ld1)
            scatter(ia0, ib0, rows0, ld0, st0)

            @pl.when(j + 2 < n_chunks)
            def _():
                load(j + 2, ia0, ib0, rows0, ld0)

            scatter(ia1, ib1, rows1, ld1, st1)

    return scatter_kernel(rows, dest)


def _split_dot(a, m_bf16):
    hi = a.astype(BF16)
    lo = (a - hi.astype(F32)).astype(BF16)
    return (jnp.dot(hi, m_bf16, preferred_element_type=F32)
            + jnp.dot(lo, m_bf16, preferred_element_type=F32))


def _in_proj_kernel(x_ref, g_ref, w_ref, cos_ref, sina_ref, sinb_ref, u_ref):
    x = x_ref[...]
    h = x * lax.rsqrt(jnp.mean(x * x, axis=-1, keepdims=True) + EPS) * g_ref[...]
    u = jnp.dot(h.astype(BF16), w_ref[...], preferred_element_type=F32)
    u_ref[:, :COL_SWA_Q] = u[:, :COL_SWA_Q].astype(BF16)
    cosf, sina, sinb = cos_ref[...], sina_ref[...], sinb_ref[...]
    for c in range(COL_SWA_Q, COL_SWA_KV + 128, 128):
        t = u[:, c:c + 128]
        fwd = pltpu.roll(t, 128 - ROPE_DIM // 2, axis=1)
        bwd = pltpu.roll(t, ROPE_DIM // 2, axis=1)
        u_ref[:, c:c + 128] = (t * cosf + fwd * sina + bwd * sinb).astype(BF16)
    u_ref[:, COL_SWA_KV + 128:] = u[:, COL_SWA_KV + 128:].astype(BF16)


def _in_proj(x, g, w_bf16, rope, seq):
    n = x.shape[0]
    tiles_per_seq = seq // TM
    cosf, sina, sinb = rope
    rope_spec = pl.BlockSpec((TM, 128), lambda i: (i % tiles_per_seq, 0))
    return pl.pallas_call(
        _in_proj_kernel,
        grid=(n // TM,),
        in_specs=[pl.BlockSpec((TM, D_MODEL), lambda i: (i, 0)),
                  pl.BlockSpec((1, D_MODEL), lambda i: (0, 0)),
                  pl.BlockSpec((D_MODEL, IN_WIDTH), lambda i: (0, 0)),
                  rope_spec, rope_spec, rope_spec],
        out_specs=pl.BlockSpec((TM, IN_WIDTH), lambda i: (i, 0)),
        out_shape=jax.ShapeDtypeStruct((n, IN_WIDTH), BF16),
        compiler_params=_cparams("parallel"),
        name="in_proj",
    )(x, g.reshape(1, D_MODEL), w_bf16, cosf, sina, sinb)


def _rope_tables(seq):
    half = ROPE_DIM // 2
    inv = jnp.float32(ROPE_THETA) ** (-jnp.arange(half, dtype=F32) * 2.0 / ROPE_DIM)
    ang = jnp.arange(seq).astype(F32)[:, None] * inv[None, :]
    cos, sin = jnp.cos(ang), jnp.sin(ang)
    ones = jnp.ones((seq, HEAD_DIM - ROPE_DIM), F32)
    zeros8 = jnp.zeros((seq, half), F32)
    zeros48 = jnp.zeros((seq, HEAD_DIM - ROPE_DIM), F32)
    cosf = jnp.concatenate([cos, cos, ones], -1)
    sina = jnp.concatenate([-sin, zeros8, zeros48], -1)
    sinb = jnp.concatenate([zeros8, sin, zeros48], -1)
    return tuple(jnp.tile(t, (1, 128 // HEAD_DIM)) for t in (cosf, sina, sinb))


CONV_CHUNK = 128


def _conv_kernel(u_ref, w_ref, b_ref, ng_ref, nb_ref, avg_ref, o_ref, vpad_ref, *, seq):
    zeros = jnp.zeros((CONV_HALO, GROUP_WIDTH), F32)
    vpad_ref[0:CONV_HALO, :] = zeros
    vpad_ref[CONV_HALO + seq:CONV_HALO + seq + CONV_HALO, :] = zeros

    def glu(i, carry):
        r = pl.multiple_of(i * CONV_CHUNK, CONV_CHUNK)
        blk = u_ref[pl.ds(r, CONV_CHUNK), :].astype(F32)
        a, g = blk[:, :GROUP_WIDTH], blk[:, GROUP_WIDTH:]
        vpad_ref[pl.ds(CONV_HALO + r, CONV_CHUNK), :] = a / (1.0 + jnp.exp(-g))
        return carry

    lax.fori_loop(0, seq // CONV_CHUNK, glu, 0)
    avg = avg_ref[...]
    bias, ng, nb = b_ref[...], ng_ref[...], nb_ref[...]

    def conv(i, carry):
        r = pl.multiple_of(i * CONV_CHUNK, CONV_CHUNK)
        acc = jnp.zeros((CONV_CHUNK, GROUP_WIDTH), F32) + bias
        win_rows = CONV_CHUNK + 2 * CONV_HALO
        win = vpad_ref[pl.ds(r, win_rows), :]
        for b in range(8):
            shifted = win if b == 0 else pltpu.roll(win, win_rows - b, axis=0)
            for a in range(4):
                j = 8 * a + b - (CONV_HALO - CONV_WIDTH // 2)
                if 0 <= j < CONV_WIDTH:
                    acc = acc + w_ref[j:j + 1, :] * shifted[8 * a:8 * a + CONV_CHUNK, :]
        mu = _split_dot(acc, avg)
        d = acc - mu
        var = _split_dot(d * d, avg)
        y = d * lax.rsqrt(var + EPS) * ng + nb
        o_ref[pl.ds(r, CONV_CHUNK), :] = (y / (1.0 + jnp.exp(-y))).astype(BF16)
        return carry

    lax.fori_loop(0, seq // CONV_CHUNK, conv, 0)


def _conv_mixer(u, conv_w, conv_b, norm_g, norm_b, nseq, seq):
    gid = np.arange(GROUP_WIDTH) // HEAD_DIM
    avg = jnp.asarray((gid[:, None] == gid[None, :]).astype(np.float32) / HEAD_DIM, BF16)
    row = lambda v: v.reshape(1, GROUP_WIDTH)
    const = lambda shape: pl.BlockSpec(shape, lambda b: (0, 0))
    return pl.pallas_call(
        functools.partial(_conv_kernel, seq=seq),
        grid=(nseq,),
        in_specs=[pl.BlockSpec((seq, 2 * GROUP_WIDTH), lambda b: (b, COL_CONV // (2 * GROUP_WIDTH))),
                  const((CONV_WIDTH, GROUP_WIDTH)), const((1, GROUP_WIDTH)), const((1, GROUP_WIDTH)),
                  const((1, GROUP_WIDTH)), const((GROUP_WIDTH, GROUP_WIDTH))],
        out_specs=pl.BlockSpec((seq, GROUP_WIDTH), lambda b: (b, 0)),
        out_shape=jax.ShapeDtypeStruct((nseq * seq, GROUP_WIDTH), BF16),
        scratch_shapes=[pltpu.VMEM((seq + 2 * CONV_HALO, GROUP_WIDTH), F32)],
        compiler_params=_cparams("parallel"),
        name="conv_mixer",
    )(u, conv_w, row(conv_b), row(norm_g), row(norm_b), avg)


SWA_TQ = 256


def _swa_kernel(sink_ref, q_ref, kv_ref, o_ref, *, seq):
    i = pl.program_id(1)
    kw = SWA_TQ + 2 * WINDOW
    kstart = pl.multiple_of(jnp.clip(i * SWA_TQ - WINDOW, 0, seq - kw), WINDOW)
    kv = kv_ref[pl.ds(kstart, kw), :]
    q = q_ref[...]
    qpos = i * SWA_TQ + lax.broadcasted_iota(jnp.int32, (SWA_TQ, kw), 0)
    kpos = kstart + lax.broadcasted_iota(jnp.int32, (SWA_TQ, kw), 1)
    valid = jnp.abs(qpos - kpos) <= WINDOW
    rep = SWA_HEADS // SWA_KV_HEADS
    for g in range(SWA_KV_HEADS):
        k = kv[:, g * HEAD_DIM:(g + 1) * HEAD_DIM]
        v = kv[:, SWA_KV_HEADS * HEAD_DIM + g * HEAD_DIM:SWA_KV_HEADS * HEAD_DIM + (g + 1) * HEAD_DIM]
        for r in range(rep):
            h = g * rep + r
            qh = q[:, h * HEAD_DIM:(h + 1) * HEAD_DIM]
            s = lax.dot_general(qh, k, (((1,), (1,)), ((), ())), preferred_element_type=F32)
            s = jnp.where(valid, s * (HEAD_DIM ** -0.5), NEG)
            sink = sink_ref[h]
            m = jnp.maximum(jnp.max(s, axis=-1, keepdims=True), sink)
            p = jnp.exp(s - m)
            denom = jnp.sum(p, axis=-1, keepdims=True) + jnp.exp(sink - m)
            o = jnp.dot(p.astype(BF16), v, preferred_element_type=F32) / denom
            o_ref[:, h * HEAD_DIM:(h + 1) * HEAD_DIM] = o.astype(BF16)


def _swa_mixer(u, sink, nseq, seq):
    nq = seq // SWA_TQ
    return pl.pallas_call(
        functools.partial(_swa_kernel, seq=seq),
        grid=(nseq, nq),
        in_specs=[pl.BlockSpec(memory_space=pltpu.SMEM),
                  pl.BlockSpec((SWA_TQ, GROUP_WIDTH), lambda b, i: (b * nq + i, COL_SWA_Q // GROUP_WIDTH)),
                  pl.BlockSpec((seq, GROUP_WIDTH), lambda b, i: (b, COL_SWA_KV // GROUP_WIDTH))],
        out_specs=pl.BlockSpec((SWA_TQ, GROUP_WIDTH), lambda b, i: (b * nq + i, 0)),
        out_shape=jax.ShapeDtypeStruct((nseq * seq, GROUP_WIDTH), BF16),
        compiler_params=_cparams("parallel", "arbitrary"),
        name="swa_mixer",
    )(sink, u, u)


def _fnet_kernel(x_ref, m_ref, cc_ref, sc_ref, o_ref, z_ref, *, seq):
    r1 = seq // FNET_S2
    half = r1 // 2
    xs = [x_ref[s1 * FNET_S2:(s1 + 1) * FNET_S2, :].astype(F32) for s1 in range(r1)]
    sym = {s1: xs[s1] + xs[r1 - s1] for s1 in range(1, half)}
    asym = {s1: xs[s1] - xs[r1 - s1] for s1 in range(1, half)}

    def weighted(terms):
        acc = None
        for w, t in terms:
            if abs(w) < 1e-9:
                continue
            if abs(w - 1.0) < 1e-9:
                acc = t if acc is None else acc + t
            elif abs(w + 1.0) < 1e-9:
                acc = -t if acc is None else acc - t
            else:
                acc = w * t if acc is None else acc + w * t
        return jnp.zeros((FNET_S2, GROUP_WIDTH), F32) if acc is None else acc

    for k1 in range(half + 1):
        re_terms = [(1.0, xs[0]), (float((-1) ** k1), xs[half])]
        re_terms += [(math.cos(2 * math.pi * k1 * s1 / r1), sym[s1]) for s1 in range(1, half)]
        im_terms = [(-math.sin(2 * math.pi * k1 * s1 / r1), asym[s1]) for s1 in range(1, half)]
        zr = weighted(re_terms).astype(BF16)
        zi = weighted(im_terms)
        z_ref[k1, :FNET_S2, :] = zr
        z_ref[k1, FNET_S2:, :] = zi.astype(BF16)
        if 0 < k1 < half:
            z_ref[r1 - k1, :FNET_S2, :] = zr
            z_ref[r1 - k1, FNET_S2:, :] = (-zi).astype(BF16)

    scale = 1.0 / math.sqrt(seq * FNET_CH)
    cc, sc = cc_ref[...], sc_ref[...]
    for k1 in range(r1):
        y = jnp.dot(m_ref[k1], z_ref[k1], preferred_element_type=F32)
        out = (jnp.dot(y[:FNET_S2].astype(BF16), cc, preferred_element_type=F32)
               + jnp.dot(y[FNET_S2:].astype(BF16), sc, preferred_element_type=F32))
        o_ref[:, k1 * GROUP_WIDTH:(k1 + 1) * GROUP_WIDTH] = (out * scale).astype(BF16)


def _fnet_constants(seq):
    r1 = seq // FNET_S2
    k = (np.arange(r1)[:, None, None] + r1 * np.arange(FNET_S2)[None, :, None]).astype(np.float64)
    s2 = np.arange(FNET_S2)[None, None, :].astype(np.float64)
    ang = 2 * np.pi * ((k * s2) % seq) / seq
    c, s = np.cos(ang), np.sin(ang)
    m = np.concatenate([np.concatenate([c, s], -1), np.concatenate([-s, c], -1)], 1)
    ch = np.arange(GROUP_WIDTH)
    same = (ch[:, None] // FNET_CH) == (ch[None, :] // FNET_CH)
    ang_c = 2 * np.pi * ((ch[:, None] % FNET_CH) * (ch[None, :] % FNET_CH) % FNET_CH) / FNET_CH
    cc = np.where(same, np.cos(ang_c), 0.0)
    sc = np.where(same, np.sin(ang_c), 0.0)
    return (jnp.asarray(m, BF16), jnp.asarray(cc, BF16), jnp.asarray(sc, BF16))


def _fnet_mixer(u, consts, nseq, seq):
    r1 = seq // FNET_S2
    m, cc, sc = consts
    out = pl.pallas_call(
        functools.partial(_fnet_kernel, seq=seq),
        grid=(nseq,),
        in_specs=[pl.BlockSpec((seq, GROUP_WIDTH), lambda b: (b, COL_FNET // GROUP_WIDTH)),
                  pl.BlockSpec((r1, 2 * FNET_S2, 2 * FNET_S2), lambda b: (0, 0, 0)),
                  pl.BlockSpec((GROUP_WIDTH, GROUP_WIDTH), lambda b: (0, 0)),
                  pl.BlockSpec((GROUP_WIDTH, GROUP_WIDTH), lambda b: (0, 0))],
        out_specs=pl.BlockSpec((FNET_S2, r1 * GROUP_WIDTH), lambda b: (b, 0)),
        out_shape=jax.ShapeDtypeStruct((nseq * FNET_S2, r1 * GROUP_WIDTH), BF16),
        scratch_shapes=[pltpu.VMEM((r1, 2 * FNET_S2, GROUP_WIDTH), BF16)],
        compiler_params=_cparams("parallel"),
        name="fnet_mixer",
    )(u, m, cc, sc)
    return out.reshape(nseq * seq, GROUP_WIDTH)


NAT_G = 4


def _nat_geometry(rows, kh):
    wr = NAT_G + kh
    n_groups = rows // NAT_G
    w0 = np.clip(NAT_G * np.arange(n_groups) - kh // 2, 0, rows - wr)
    patterns, pat_id = [], []
    for g in range(n_groups):
        r = NAT_G * g + np.arange(NAT_G)
        kr0 = np.clip(r - kh // 2, 0, rows - kh)
        kr = w0[g] + np.arange(wr)
        valid = (kr[None, :] >= kr0[:, None]) & (kr[None, :] < kr0[:, None] + kh)
        dr = np.where(valid, kr[None, :] - r[:, None] + NAT_KH - 1, 0)
        assert valid.sum(1).min() == kh and dr.min() >= 0 and dr.max() <= 2 * NAT_KH - 2
        for i, (v, d) in enumerate(patterns):
            if np.array_equal(v, valid) and np.array_equal(d, dr):
                pat_id.append(i)
                break
        else:
            pat_id.append(len(patterns))
            patterns.append((valid, dr))
    return w0.astype(np.int32), np.asarray(pat_id, np.int32), patterns


def _nat_bias_kernel(rpb_ref, onehot_ref, o_ref):
    a = rpb_ref[...]
    a1 = a.astype(BF16)
    r1 = a - a1.astype(F32)
    a2 = r1.astype(BF16)
    a3 = (r1 - a2.astype(F32)).astype(BF16)
    e = onehot_ref[...]
    o_ref[...] = (jnp.dot(a1, e, preferred_element_type=F32) + jnp.dot(a2, e, preferred_element_type=F32)
                  + jnp.dot(a3, e, preferred_element_type=F32))


def _nat_bias_table(rpb, patterns):
    n_dr, n_dc = 2 * NAT_KH - 1, 2 * NAT_KW - 1
    c = np.arange(GRID_W)
    dc = np.clip(c[None, :] - c[:, None] + (NAT_KW - 1), 0, n_dc - 1)
    onehot = (np.arange(n_dc)[:, None, None] == dc[None]).astype(np.float32)
    onehot = np.concatenate([onehot.reshape(n_dc, -1), np.zeros((32 - n_dc, GRID_W * GRID_W), np.float32)], 0)
    rows = NAT_HEADS * n_dr
    rows_pad = -(-rows // 8) * 8
    rpb2 = jnp.zeros((rows_pad, 32), F32).at[:rows, :n_dc].set(rpb.reshape(rows, n_dc))
    t = pl.pallas_call(
        _nat_bias_kernel,
        out_shape=jax.ShapeDtypeStruct((rows_pad, GRID_W * GRID_W), F32),
        name="nat_bias",
    )(rpb2, jnp.asarray(onehot, BF16))
    t = t[:rows].reshape(NAT_HEADS, n_dr, GRID_W, GRID_W)
    c0 = np.clip(c - NAT_KW // 2, 0, GRID_W - NAT_KW)
    col_ok = (c[None, :] >= c0[:, None]) & (c[None, :] < c0[:, None] + NAT_KW)
    mask = jnp.asarray(np.where(col_ok, 0.0, NEG), F32)
    tabs = []
    for valid, dr in patterns:
        wr = valid.shape[1]
        win = t[:, dr] + mask[None, None, None]
        win = jnp.where(jnp.asarray(valid)[None, :, :, None, None], win, NEG)
        win = jnp.transpose(win, (0, 1, 3, 2, 4))
        tabs.append(win.reshape(NAT_HEADS, NAT_G * GRID_W, wr * GRID_W))
    return jnp.stack(tabs, 0)


def _nat_kernel(w0_ref, pat_ref, q_ref, k_ref, v_ref, bias_ref, o_ref, *, wk):
    g = pl.program_id(1)
    kbase = pl.multiple_of(w0_ref[g] * GRID_W, GRID_W)
    kk = k_ref[pl.ds(kbase, wk), :]
    vv = v_ref[pl.ds(kbase, wk), :]
    q = q_ref[...]
    tq = q.shape[0]
    lane = lax.broadcasted_iota(jnp.int32, (tq, 128), 1)
    upper, lower = lane >= HEAD_DIM, lane < HEAD_DIM
    for hp in range(NAT_HEADS // 2):
        sl = slice(hp * 128, (hp + 1) * 128)
        q2, k2, v2 = q[:, sl], kk[:, sl], vv[:, sl]
        outs = []
        for r in range(2):
            qm = jnp.where(upper if r else lower, q2, jnp.zeros_like(q2))
            s = lax.dot_general(qm, k2, (((1,), (1,)), ((), ())), preferred_element_type=F32)
            s = s * (HEAD_DIM ** -0.5) + bias_ref[0, 2 * hp + r]
            m = jnp.max(s, axis=-1, keepdims=True)
            p = jnp.exp(s - m)
            denom = jnp.sum(p, axis=-1, keepdims=True)
            outs.append(jnp.dot(p.astype(BF16), v2, preferred_element_type=F32) / denom)
        o_ref[:, sl] = jnp.where(upper, outs[1], outs[0]).astype(BF16)


def _nat_mixer(u, rpb, nseq, seq):
    rows = seq // GRID_W
    kh = min(NAT_KH, rows)
    w0, pat_id, patterns = _nat_geometry(rows, kh)
    bias_tab = _nat_bias_table(rpb, patterns)
    n_groups = rows // NAT_G
    tq = NAT_G * GRID_W
    wk = (NAT_G + kh) * GRID_W
    kv_spec = lambda col: pl.BlockSpec((seq, GROUP_WIDTH), lambda b, g, w0, pat: (b, col // GROUP_WIDTH))
    grid_spec = pltpu.PrefetchScalarGridSpec(
        num_scalar_prefetch=2,
        grid=(nseq, n_groups),
        in_specs=[pl.BlockSpec((tq, GROUP_WIDTH),
                               lambda b, g, w0, pat: (b * n_groups + g, COL_NAT_Q // GROUP_WIDTH)),
                  kv_spec(COL_NAT_K), kv_spec(COL_NAT_V),
                  pl.BlockSpec((1, NAT_HEADS, tq, wk), lambda b, g, w0, pat: (pat[g], 0, 0, 0))],
        out_specs=pl.BlockSpec((tq, GROUP_WIDTH), lambda b, g, w0, pat: (b * n_groups + g, 0)),
    )
    return pl.pallas_call(
        functools.partial(_nat_kernel, wk=wk),
        grid_spec=grid_spec,
        out_shape=jax.ShapeDtypeStruct((nseq * seq, GROUP_WIDTH), BF16),
        compiler_params=_cparams("parallel", "arbitrary"),
        name="nat_mixer",
    )(jnp.asarray(w0), jnp.asarray(pat_id), u, u, u, bias_tab)


ROUTE_W = 128


def _out_proj_kernel(ya_ref, yb_ref, yc_ref, yd_ref, x_ref, mg_ref, wo_ref, g2_ref, wr_ref, br_ref, tri_ref,
                     x2_ref, h2_ref, route_ref, count_ref, base_ref):
    @pl.when(pl.program_id(0) == 0)
    def _():
        base_ref[...] = jnp.zeros_like(base_ref)

    acc = x_ref[...]
    for j, y_ref in enumerate((ya_ref, yb_ref, yc_ref, yd_ref)):
        y = y_ref[...].astype(F32)
        y = y * lax.rsqrt(jnp.mean(y * y, axis=-1, keepdims=True) + EPS)
        y = y * mg_ref[:, j * GROUP_WIDTH:(j + 1) * GROUP_WIDTH]
        acc = acc + jnp.dot(y.astype(BF16), wo_ref[j * GROUP_WIDTH:(j + 1) * GROUP_WIDTH, :],
                            preferred_element_type=F32)
    x2_ref[...] = acc
    h2 = acc * lax.rsqrt(jnp.mean(acc * acc, axis=-1, keepdims=True) + EPS) * g2_ref[...]
    h2_ref[...] = h2

    a1 = h2.astype(BF16)
    a2 = (h2 - a1.astype(F32)).astype(BF16)
    w1, w2 = wr_ref[0], wr_ref[1]
    dot = lambda a, w: jnp.dot(a, w, preferred_element_type=F32)
    logits = dot(a1, w1) + (dot(a1, w2) + dot(a2, w1)) + br_ref[...]

    lane = lax.broadcasted_iota(jnp.int32, logits.shape, 1).astype(F32)
    big = jnp.float32(ROUTE_W)
    is_g = lane < N_EXPERT_GROUPS
    gl = jnp.where(is_g, logits, NEG)
    gm = jnp.max(gl, axis=-1, keepdims=True)
    grp = jnp.min(jnp.where(gl == gm, lane, big), axis=-1, keepdims=True)
    gsum = jnp.sum(jnp.where(is_g, jnp.exp(gl - gm), 0.0), axis=-1, keepdims=True)
    p_grp = 1.0 / gsum
    lo = N_EXPERT_GROUPS + EXPERTS_PER_GROUP * grp
    emask = (lane >= lo) & (lane < lo + EXPERTS_PER_GROUP)
    el = jnp.where(emask, logits, NEG)
    l1 = jnp.max(el, axis=-1, keepdims=True)
    i1 = jnp.min(jnp.where(emask & (el == l1), lane, big), axis=-1, keepdims=True)
    rest = emask & (lane != i1)
    el2 = jnp.where(rest, logits, NEG)
    l2 = jnp.max(el2, axis=-1, keepdims=True)
    i2 = jnp.min(jnp.where(rest & (el2 == l2), lane, big), axis=-1, keepdims=True)
    e21 = jnp.exp(l2 - l1)
    gate0 = p_grp / (1.0 + e21)
    gate1 = p_grp * e21 / (1.0 + e21)
    e0 = i1 - N_EXPERT_GROUPS
    e1 = i2 - N_EXPERT_GROUPS

    hit0 = lane == e0
    hit1 = lane == e1
    onehot = jnp.where(hit0 | hit1, 1.0, 0.0)
    before = jnp.dot(tri_ref[...], onehot.astype(BF16), preferred_element_type=F32) + base_ref[0:1, :]
    rank0 = jnp.sum(jnp.where(hit0, before, 0.0), axis=-1, keepdims=True)
    rank1 = jnp.sum(jnp.where(hit1, before, 0.0), axis=-1, keepdims=True)
    new_base = base_ref[0:1, :] + jnp.sum(onehot, axis=0, keepdims=True)
    base_ref[...] = jnp.broadcast_to(new_base, base_ref.shape)
    count_ref[...] = jnp.broadcast_to(new_base, count_ref.shape)

    rec = jnp.zeros_like(logits)
    for col, val in enumerate((e0, e1, gate0, gate1, rank0, rank1)):
        rec = jnp.where(lane == col, val, rec)
    route_ref[...] = rec


def _out_proj(ys, x, mix_g, wo_bf16, g2, wr3, br, tri):
    n = x.shape[0]
    tile = lambda w: pl.BlockSpec((TM, w), lambda i: (i, 0))
    const = lambda *shape: pl.BlockSpec(shape, lambda i: (0,) * len(shape))
    return pl.pallas_call(
        _out_proj_kernel,
        grid=(n // TM,),
        in_specs=[tile(GROUP_WIDTH)] * 4 + [tile(D_MODEL), const(1, D_MODEL), const(D_MODEL, D_MODEL),
                                           const(1, D_MODEL), const(2, D_MODEL, ROUTE_W), const(1, ROUTE_W),
                                           const(TM, TM)],
        out_specs=[tile(D_MODEL), tile(D_MODEL), tile(ROUTE_W), const(8, ROUTE_W)],
        out_shape=[jax.ShapeDtypeStruct((n, D_MODEL), F32), jax.ShapeDtypeStruct((n, D_MODEL), F32),
                   jax.ShapeDtypeStruct((n, ROUTE_W), F32), jax.ShapeDtypeStruct((8, ROUTE_W), F32)],
        scratch_shapes=[pltpu.VMEM((8, ROUTE_W), F32)],
        compiler_params=_cparams("arbitrary"),
        name="out_proj_router",
    )(*ys, x, mix_g.reshape(1, D_MODEL), wo_bf16, g2.reshape(1, D_MODEL), wr3, br, tri)


def _router_weights(wg, bg, we, be):
    w = jnp.concatenate([wg, we, jnp.zeros((D_MODEL, ROUTE_W - N_EXPERT_GROUPS - N_EXPERTS), F32)], -1)
    w1 = w.astype(BF16)
    w2 = (w - w1.astype(F32)).astype(BF16)
    b = jnp.concatenate([bg, be, jnp.zeros((ROUTE_W - N_EXPERT_GROUPS - N_EXPERTS,), F32)]).reshape(1, ROUTE_W)
    return jnp.stack([w1, w2], 0), b


def _moe_kernel(be_ref, valid_ref, x_ref, wg_ref, wu_ref, wd_ref, o_ref, wg_bf, wu_bf, wd_bf):
    b = pl.program_id(0)

    @pl.when(jnp.logical_or(b == 0, be_ref[b] != be_ref[jnp.maximum(b - 1, 0)]))
    def _():
        wg_bf[...] = wg_ref[0].astype(BF16)
        wu_bf[...] = wu_ref[0].astype(BF16)
        wd_bf[...] = wd_ref[0].astype(BF16)

    row = lax.broadcasted_iota(jnp.int32, (MOE_BLOCK, 1), 0)
    x = jnp.where(row < valid_ref[b], x_ref[...], 0.0).astype(BF16)
    gate = jnp.dot(x, wg_bf[...], preferred_element_type=F32)
    up = jnp.dot(x, wu_bf[...], preferred_element_type=F32)
    hid = (gate / (1.0 + jnp.exp(-gate))) * up
    o_ref[...] = jnp.dot(hid.astype(BF16), wd_bf[...], preferred_element_type=F32)


def _moe_blocks(block_expert, block_valid, x_sorted, wg, wu, wd):
    nb = block_expert.shape[0]
    wspec = lambda shape: pl.BlockSpec((1,) + shape, lambda b, be, valid: (be[b], 0, 0))
    grid_spec = pltpu.PrefetchScalarGridSpec(
        num_scalar_prefetch=2,
        grid=(nb,),
        in_specs=[pl.BlockSpec((MOE_BLOCK, D_MODEL), lambda b, be, valid: (b, 0)),
                  wspec((D_MODEL, D_EXPERT)), wspec((D_MODEL, D_EXPERT)), wspec((D_EXPERT, D_MODEL))],
        out_specs=pl.BlockSpec((MOE_BLOCK, D_MODEL), lambda b, be, valid: (b, 0)),
        scratch_shapes=[pltpu.VMEM((D_MODEL, D_EXPERT), BF16), pltpu.VMEM((D_MODEL, D_EXPERT), BF16),
                        pltpu.VMEM((D_EXPERT, D_MODEL), BF16)],
    )
    return pl.pallas_call(
        _moe_kernel,
        grid_spec=grid_spec,
        out_shape=jax.ShapeDtypeStruct((nb * MOE_BLOCK, D_MODEL), F32),
        compiler_params=_cparams("arbitrary"),
        name="moe_blocks",
    )(block_expert, block_valid, x_sorted, wg, wu, wd)


def _combine_kernel(x_ref, route_ref, g_ref, y0_ref, y1_ref, o_ref, *, final_norm):
    route = route_ref[...]
    x = x_ref[...] + route[:, 2:3] * y0_ref[...] + route[:, 3:4] * y1_ref[...]
    if final_norm:
        x = x * lax.rsqrt(jnp.mean(x * x, axis=-1, keepdims=True) + EPS) * g_ref[...]
    o_ref[...] = x


def _combine(x2, route, y_rows, final_g, final_norm):
    n = x2.shape[0]
    n_tiles = n // TM
    return pl.pallas_call(
        functools.partial(_combine_kernel, final_norm=final_norm),
        grid=(n_tiles,),
        in_specs=[pl.BlockSpec((TM, D_MODEL), lambda i: (i, 0)),
                  pl.BlockSpec((TM, ROUTE_W), lambda i: (i, 0)),
                  pl.BlockSpec((1, D_MODEL), lambda i: (0, 0)),
                  pl.BlockSpec((TM, D_MODEL), lambda i: (i, 0)),
                  pl.BlockSpec((TM, D_MODEL), lambda i: (i + n_tiles, 0))],
        out_specs=pl.BlockSpec((TM, D_MODEL), lambda i: (i, 0)),
        out_shape=jax.ShapeDtypeStruct((n, D_MODEL), F32),
        compiler_params=_cparams("parallel"),
        name="moe_combine",
    )(x2, route, final_g.reshape(1, D_MODEL), y_rows, y_rows)


def _routing_tables(route, counts, n):
    e = route[:, 0:2].astype(jnp.int32)
    rank = route[:, 4:6].astype(jnp.int32)
    cnt = counts[0, :N_EXPERTS].astype(jnp.int32)
    padded = (cnt + MOE_BLOCK - 1) // MOE_BLOCK * MOE_BLOCK
    pend = jnp.cumsum(padded)
    pstart = pend - padded
    dest = pstart[e] + rank
    n_blocks = -(-(2 * n) // MOE_BLOCK) + N_EXPERTS
    block_first = jnp.arange(n_blocks, dtype=jnp.int32) * MOE_BLOCK
    block_expert = jnp.minimum(jnp.sum((pend[None, :] <= block_first[:, None]).astype(jnp.int32), axis=1),
                               N_EXPERTS - 1)
    block_valid = jnp.clip((pstart + cnt)[block_expert] - block_first, 0, MOE_BLOCK)
    dest_flat = jnp.transpose(dest, (1, 0)).reshape(2 * n)
    return block_expert, block_valid, dest_flat


def _trunk(x, p, nseq, seq):
    n = nseq * seq
    depth = p['w_in'].shape[0]
    rope = _rope_tables(seq)
    fnet_consts = _fnet_constants(seq)
    tri = jnp.asarray(np.tril(np.ones((TM, TM), np.float32), -1), BF16)
    final_g = p['final_norm_g']
    layers = {k: v for k, v in p.items() if k != 'final_norm_g'}

    def layer(x, w):
        u = _in_proj(x, w['norm1_g'], w['w_in'].astype(BF16), rope, seq)
        ya = _conv_mixer(u, w['conv_w'], w['conv_b'], w['conv_norm_g'], w['conv_norm_b'], nseq, seq)
        yb = _swa_mixer(u, w['attn_sink'], nseq, seq)
        yc = _fnet_mixer(u, fnet_consts, nseq, seq)
        yd = _nat_mixer(u, w['nat_rpb'], nseq, seq)
        wr3, br = _router_weights(w['router_group_w'], w['router_group_b'],
                                  w['router_expert_w'], w['router_expert_b'])
        x2, h2, route, counts = _out_proj((ya, yb, yc, yd), x, w['mix_norm_g'], w['w_out'].astype(BF16),
                                          w['norm2_g'], wr3, br, tri)
        block_expert, block_valid, dest_flat = _routing_tables(route, counts, n)
        x_sorted = _sc_scatter_pairs(h2, dest_flat, block_expert.shape[0] * MOE_BLOCK)
        out_sorted = _moe_blocks(block_expert, block_valid, x_sorted, w['expert_w_gate'], w['expert_w_up'],
                                 w['expert_w_down'])
        return x2, route, _sc_gather(out_sorted, dest_flat)

    def body(x, w):
        x2, route, y_rows = layer(x, w)
        return _combine(x2, route, y_rows, final_g, final_norm=False), None

    head = {k: v[:depth - 1] for k, v in layers.items()}
    last = {k: v[depth - 1] for k, v in layers.items()}
    if depth > 1:
        x, _ = lax.scan(body, x, head)
    x2, route, y_rows = layer(x, last)
    return _combine(x2, route, y_rows, final_g, final_norm=True)


@jax.jit
def _forward(x_prompt, x_sample, p):
    bp, seq, d = x_prompt.shape
    bs = x_sample.shape[0]
    assert x_sample.shape[1] == seq and d == D_MODEL
    x = jnp.concatenate([x_prompt.reshape(bp * seq, d), x_sample.reshape(bs * seq, d)], 0)
    y = _trunk(x, p, bp + bs, seq)
    return (y[:bp * seq].reshape(bp, seq, d), y[bp * seq:].reshape(bs, seq, d))


def kernel(x_prompt, x_sample, norm1_g, w_in, conv_w, conv_b, conv_norm_g, conv_norm_b, attn_sink, nat_rpb,
           mix_norm_g, w_out, norm2_g, router_group_w, router_group_b, router_expert_w, router_expert_b,
           expert_w_gate, expert_w_up, expert_w_down, final_norm_g):
    p = dict(norm1_g=norm1_g, w_in=w_in, conv_w=conv_w, conv_b=conv_b, conv_norm_g=conv_norm_g,
             conv_norm_b=conv_norm_b, attn_sink=attn_sink, nat_rpb=nat_rpb, mix_norm_g=mix_norm_g, w_out=w_out,
             norm2_g=norm2_g, router_group_w=router_group_w, router_group_b=router_group_b,
             router_expert_w=router_expert_w, router_expert_b=router_expert_b, expert_w_gate=expert_w_gate,
             expert_w_up=expert_w_up, expert_w_down=expert_w_down, final_norm_g=final_norm_g)
    return _forward(x_prompt, x_sample, p)
```

```python
import functools
import math

import numpy as np
import jax
import jax.numpy as jnp
from jax import lax
from jax.experimental import pallas as pl
from jax.experimental.pallas import tpu as pltpu
from jax.experimental.pallas import tpu_sc as plsc

F32 = jnp.float32
BF16 = jnp.bfloat16

D_MODEL = 1024
GROUP_WIDTH = 256
HEAD_DIM = 64
CONV_WIDTH = 31
CONV_HALO = 16
SWA_HEADS = 4
SWA_KV_HEADS = 2
WINDOW = 128
ROPE_THETA = 500000.0
ROPE_DIM = 16
FNET_CH = 64
FNET_S2 = 256
NAT_HEADS = 4
GRID_W = 64
NAT_KH = 8
NAT_KW = 16
N_EXPERT_GROUPS = 4
EXPERTS_PER_GROUP = 8
N_EXPERTS = 32
D_EXPERT = 512
EPS = 1e-6
IN_WIDTH = 2048
NEG = -1e30

COL_CONV = 0
COL_SWA_Q = 512
COL_SWA_KV = 768
COL_FNET = 1024
COL_NAT_Q = 1280
COL_NAT_K = 1536
COL_NAT_V = 1792

TM = 512
MOE_BLOCK = 512
VMEM_LIMIT = 56 * 1024 * 1024


def _cparams(*sem):
    return pltpu.CompilerParams(dimension_semantics=sem, vmem_limit_bytes=VMEM_LIMIT)


SC_CHUNK = 32


def _sc_gather(table, idx):
    n_rows, width = idx.shape[0], table.shape[1]
    info = plsc.get_sparse_core_info()
    n_cores, n_workers = info.num_cores, info.num_cores * info.num_subcores
    per_worker = n_rows // n_workers
    n_chunks = per_worker // SC_CHUNK
    assert per_worker * n_workers == n_rows and n_chunks * SC_CHUNK == per_worker and n_chunks % 2 == 0
    mesh = plsc.VectorSubcoreMesh(core_axis_name="c", subcore_axis_name="s")
    buf = lambda: pltpu.VMEM((SC_CHUNK, width), table.dtype)
    ids = lambda: pltpu.VMEM((SC_CHUNK,), jnp.int32)

    @functools.partial(
        pl.kernel, mesh=mesh, out_type=jax.ShapeDtypeStruct((n_rows, width), table.dtype),
        scratch_types=[ids(), ids(), buf(), buf(), pltpu.SemaphoreType.DMA, pltpu.SemaphoreType.DMA])
    def gather_kernel(table_hbm, idx_hbm, out_hbm, idx0, idx1, rows0, rows1, sem0, sem1):
        base = (lax.axis_index("s") * n_cores + lax.axis_index("c")) * per_worker

        def fetch(chunk, idx_v, rows_v, sem):
            off = pl.multiple_of(base + chunk * SC_CHUNK, SC_CHUNK)
            pltpu.sync_copy(idx_hbm.at[pl.ds(off, SC_CHUNK)], idx_v)
            pltpu.async_copy(table_hbm.at[idx_v], rows_v, sem)

        def flush(chunk, idx_v, rows_v, sem):
            off = pl.multiple_of(base + chunk * SC_CHUNK, SC_CHUNK)
            pltpu.make_async_copy(table_hbm.at[idx_v], rows_v, sem).wait()
            pltpu.sync_copy(rows_v, out_hbm.at[pl.ds(off, SC_CHUNK)])

        fetch(0, idx0, rows0, sem0)

        @pl.loop(0, n_chunks, step=2)
        def _(j):
            fetch(j + 1, idx1, rows1, sem1)
            flush(j, idx0, rows0, sem0)

            @pl.when(j + 2 < n_chunks)
            def _():
                fetch(j + 2, idx0, rows0, sem0)

            flush(j + 1, idx1, rows1, sem1)

    return gather_kernel(table, idx)


def _sc_scatter_pairs(rows, dest, n_out):
    n_rows, width = rows.shape
    info = plsc.get_sparse_core_info()
    n_cores, n_workers = info.num_cores, info.num_cores * info.num_subcores
    per_worker = n_rows // n_workers
    n_chunks = per_worker // SC_CHUNK
    assert per_worker * n_workers == n_rows and n_chunks * SC_CHUNK == per_worker and n_chunks % 2 == 0
    chunks_total = n_rows // SC_CHUNK
    dest = dest.reshape(2 * chunks_total, SC_CHUNK)
    mesh = plsc.VectorSubcoreMesh(core_axis_name="c", subcore_axis_name="s")
    buf = lambda: pltpu.VMEM((SC_CHUNK, width), rows.dtype)
    ids = lambda: pltpu.VMEM((1, SC_CHUNK), jnp.int32)
    dma = pltpu.SemaphoreType.DMA

    @functools.partial(
        pl.kernel, mesh=mesh, out_type=jax.ShapeDtypeStruct((n_out, width), rows.dtype),
        scratch_types=[ids(), ids(), ids(), ids(), buf(), buf(), dma, dma, dma, dma])
    def scatter_kernel(rows_hbm, dest_hbm, out_hbm, ia0, ib0, ia1, ib1, rows0, rows1, ld0, ld1, st0, st1):
        first_chunk = (lax.axis_index("s") * n_cores + lax.axis_index("c")) * n_chunks

        def load(chunk, ia, ib, rows_v, ld):
            c = first_chunk + chunk
            pltpu.sync_copy(dest_hbm.at[pl.ds(c, 1)], ia)
            pltpu.sync_copy(dest_hbm.at[pl.ds(chunks_total + c, 1)], ib)
            pltpu.async_copy(rows_hbm.at[pl.ds(pl.multiple_of(c * SC_CHUNK, SC_CHUNK), SC_CHUNK)], rows_v, ld)

        def scatter(ia, ib, rows_v, ld, st):
            pltpu.make_async_copy(rows_hbm.at[pl.ds(0, SC_CHUNK)], rows_v, ld).wait()
            pltpu.async_copy(rows_v, out_hbm.at[ia.at[0]], st)
            pltpu.async_copy(rows_v, out_hbm.at[ib.at[0]], st)
            pltpu.make_async_copy(rows_v, out_hbm.at[ia.at[0]], st).wait()
            pltpu.make_async_copy(rows_v, out_hbm.at[ib.at[0]], st).wait()

        load(0, ia0, ib0, rows0, ld0)

        @pl.loop(0, n_chunks, step=2)
        def _(j):
            load(j + 1, ia1, ib1, rows1, ld1)
            scatter(ia0, ib0, rows0, ld0, st0)

            @pl.when(j + 2 < n_chunks)
            def _():
                load(j + 2, ia0, ib0, rows0, ld0)

            scatter(ia1, ib1, rows1, ld1, st1)

    return scatter_kernel(rows, dest)


def _split_dot(a, m_bf16):
    hi = a.astype(BF16)
    lo = (a - hi.astype(F32)).astype(BF16)
    return (jnp.dot(hi, m_bf16, preferred_element_type=F32)
            + jnp.dot(lo, m_bf16, preferred_element_type=F32))


def _in_proj_kernel(x_ref, g_ref, w_ref, cos_ref, sina_ref, sinb_ref, u_ref):
    _project_in(x_ref[...], g_ref, w_ref, cos_ref, sina_ref, sinb_ref, u_ref)


def _in_proj_combine_kernel(x2_ref, route_ref, y0_ref, y1_ref, g_ref, w_ref, cos_ref, sina_ref, sinb_ref,
                            x_ref, u_ref):
    route = route_ref[...]
    x = x2_ref[...] + route[:, 2:3] * y0_ref[...] + route[:, 3:4] * y1_ref[...]
    x_ref[...] = x
    _project_in(x, g_ref, w_ref, cos_ref, sina_ref, sinb_ref, u_ref)


def _project_in(x, g_ref, w_ref, cos_ref, sina_ref, sinb_ref, u_ref):
    h = x * lax.rsqrt(jnp.mean(x * x, axis=-1, keepdims=True) + EPS) * g_ref[...]
    u = jnp.dot(h.astype(BF16), w_ref[...], preferred_element_type=F32)
    u_ref[:, :COL_SWA_Q] = u[:, :COL_SWA_Q].astype(BF16)
    cosf, sina, sinb = cos_ref[...], sina_ref[...], sinb_ref[...]
    for c in range(COL_SWA_Q, COL_SWA_KV + 128, 128):
        t = u[:, c:c + 128]
        fwd = pltpu.roll(t, 128 - ROPE_DIM // 2, axis=1)
        bwd = pltpu.roll(t, ROPE_DIM // 2, axis=1)
        u_ref[:, c:c + 128] = (t * cosf + fwd * sina + bwd * sinb).astype(BF16)
    u_ref[:, COL_SWA_KV + 128:] = u[:, COL_SWA_KV + 128:].astype(BF16)


def _in_proj(x, g, w_bf16, rope, seq):
    n = x.shape[0]
    tiles_per_seq = seq // TM
    cosf, sina, sinb = rope
    rope_spec = pl.BlockSpec((TM, 128), lambda i: (i % tiles_per_seq, 0))
    return pl.pallas_call(
        _in_proj_kernel,
        grid=(n // TM,),
        in_specs=[pl.BlockSpec((TM, D_MODEL), lambda i: (i, 0)),
                  pl.BlockSpec((1, D_MODEL), lambda i: (0, 0)),
                  pl.BlockSpec((D_MODEL, IN_WIDTH), lambda i: (0, 0)),
                  rope_spec, rope_spec, rope_spec],
        out_specs=pl.BlockSpec((TM, IN_WIDTH), lambda i: (i, 0)),
        out_shape=jax.ShapeDtypeStruct((n, IN_WIDTH), BF16),
        compiler_params=_cparams("parallel"),
        name="in_proj",
    )(x, g.reshape(1, D_MODEL), w_bf16, cosf, sina, sinb)


def _in_proj_combine(x2, route, y_rows, g, w_bf16, rope, seq):
    n = x2.shape[0]
    n_tiles = n // TM
    tiles_per_seq = seq // TM
    cosf, sina, sinb = rope
    rope_spec = pl.BlockSpec((TM, 128), lambda i: (i % tiles_per_seq, 0))
    tile = lambda w: pl.BlockSpec((TM, w), lambda i: (i, 0))
    return pl.pallas_call(
        _in_proj_combine_kernel,
        grid=(n_tiles,),
        in_specs=[tile(D_MODEL), tile(ROUTE_W), tile(D_MODEL),
                  pl.BlockSpec((TM, D_MODEL), lambda i: (i + n_tiles, 0)),
                  pl.BlockSpec((1, D_MODEL), lambda i: (0, 0)),
                  pl.BlockSpec((D_MODEL, IN_WIDTH), lambda i: (0, 0)),
                  rope_spec, rope_spec, rope_spec],
        out_specs=[tile(D_MODEL), tile(IN_WIDTH)],
        out_shape=[jax.ShapeDtypeStruct((n, D_MODEL), F32), jax.ShapeDtypeStruct((n, IN_WIDTH), BF16)],
        compiler_params=_cparams("parallel"),
        name="in_proj_combine",
    )(x2, route, y_rows, y_rows, g.reshape(1, D_MODEL), w_bf16, cosf, sina, sinb)


def _rope_tables(seq):
    half = ROPE_DIM // 2
    inv = jnp.float32(ROPE_THETA) ** (-jnp.arange(half, dtype=F32) * 2.0 / ROPE_DIM)
    ang = jnp.arange(seq).astype(F32)[:, None] * inv[None, :]
    cos, sin = jnp.cos(ang), jnp.sin(ang)
    ones = jnp.ones((seq, HEAD_DIM - ROPE_DIM), F32)
    zeros8 = jnp.zeros((seq, half), F32)
    zeros48 = jnp.zeros((seq, HEAD_DIM - ROPE_DIM), F32)
    cosf = jnp.concatenate([cos, cos, ones], -1)
    sina = jnp.concatenate([-sin, zeros8, zeros48], -1)
    sinb = jnp.concatenate([zeros8, sin, zeros48], -1)
    return tuple(jnp.tile(t, (1, 128 // HEAD_DIM)) for t in (cosf, sina, sinb))


CONV_CHUNK = 128


def _conv_kernel(u_ref, w_ref, b_ref, ng_ref, nb_ref, avg_ref, o_ref, vpad_ref, *, seq):
    zeros = jnp.zeros((CONV_HALO, GROUP_WIDTH), F32)
    vpad_ref[0:CONV_HALO, :] = zeros
    vpad_ref[CONV_HALO + seq:CONV_HALO + seq + CONV_HALO, :] = zeros

    def glu(i, carry):
        r = pl.multiple_of(i * CONV_CHUNK, CONV_CHUNK)
        blk = u_ref[pl.ds(r, CONV_CHUNK), :].astype(F32)
        a, g = blk[:, :GROUP_WIDTH], blk[:, GROUP_WIDTH:]
        vpad_ref[pl.ds(CONV_HALO + r, CONV_CHUNK), :] = a / (1.0 + jnp.exp(-g))
        return carry

    lax.fori_loop(0, seq // CONV_CHUNK, glu, 0)
    avg = avg_ref[...]
    bias, ng, nb = b_ref[...], ng_ref[...], nb_ref[...]

    def conv(i, carry):
        r = pl.multiple_of(i * CONV_CHUNK, CONV_CHUNK)
        acc = jnp.zeros((CONV_CHUNK, GROUP_WIDTH), F32) + bias
        win_rows = CONV_CHUNK + 2 * CONV_HALO
        win = vpad_ref[pl.ds(r, win_rows), :]
        for b in range(8):
            shifted = win if b == 0 else pltpu.roll(win, win_rows - b, axis=0)
            for a in range(4):
                j = 8 * a + b - (CONV_HALO - CONV_WIDTH // 2)
                if 0 <= j < CONV_WIDTH:
                    acc = acc + w_ref[j:j + 1, :] * shifted[8 * a:8 * a + CONV_CHUNK, :]
        mu = _split_dot(acc, avg)
        d = acc - mu
        var = _split_dot(d * d, avg)
        y = d * lax.rsqrt(var + EPS) * ng + nb
        o_ref[pl.ds(r, CONV_CHUNK), :] = (y / (1.0 + jnp.exp(-y))).astype(BF16)
        return carry

    lax.fori_loop(0, seq // CONV_CHUNK, conv, 0)


def _conv_mixer(u, conv_w, conv_b, norm_g, norm_b, nseq, seq):
    gid = np.arange(GROUP_WIDTH) // HEAD_DIM
    avg = jnp.asarray((gid[:, None] == gid[None, :]).astype(np.float32) / HEAD_DIM, BF16)
    row = lambda v: v.reshape(1, GROUP_WIDTH)
    const = lambda shape: pl.BlockSpec(shape, lambda b: (0, 0))
    return pl.pallas_call(
        functools.partial(_conv_kernel, seq=seq),
        grid=(nseq,),
        in_specs=[pl.BlockSpec((seq, 2 * GROUP_WIDTH), lambda b: (b, COL_CONV // (2 * GROUP_WIDTH))),
                  const((CONV_WIDTH, GROUP_WIDTH)), const((1, GROUP_WIDTH)), const((1, GROUP_WIDTH)),
                  const((1, GROUP_WIDTH)), const((GROUP_WIDTH, GROUP_WIDTH))],
        out_specs=pl.BlockSpec((seq, GROUP_WIDTH), lambda b: (b, 0)),
        out_shape=jax.ShapeDtypeStruct((nseq * seq, GROUP_WIDTH), BF16),
        scratch_shapes=[pltpu.VMEM((seq + 2 * CONV_HALO, GROUP_WIDTH), F32)],
        compiler_params=_cparams("parallel"),
        name="conv_mixer",
    )(u, conv_w, row(conv_b), row(norm_g), row(norm_b), avg)


SWA_TQ = 256


def _swa_kernel(sink_ref, q_ref, kv_ref, o_ref, *, seq):
    i = pl.program_id(1)
    kw = SWA_TQ + 2 * WINDOW
    kstart = pl.multiple_of(jnp.clip(i * SWA_TQ - WINDOW, 0, seq - kw), WINDOW)
    kv = kv_ref[pl.ds(kstart, kw), :]
    q = q_ref[...]
    qpos = i * SWA_TQ + lax.broadcasted_iota(jnp.int32, (SWA_TQ, kw), 0)
    kpos = kstart + lax.broadcasted_iota(jnp.int32, (SWA_TQ, kw), 1)
    valid = jnp.abs(qpos - kpos) <= WINDOW
    rep = SWA_HEADS // SWA_KV_HEADS
    for g in range(SWA_KV_HEADS):
        k = kv[:, g * HEAD_DIM:(g + 1) * HEAD_DIM]
        v = kv[:, SWA_KV_HEADS * HEAD_DIM + g * HEAD_DIM:SWA_KV_HEADS * HEAD_DIM + (g + 1) * HEAD_DIM]
        for r in range(rep):
            h = g * rep + r
            qh = q[:, h * HEAD_DIM:(h + 1) * HEAD_DIM]
            s = lax.dot_general(qh, k, (((1,), (1,)), ((), ())), preferred_element_type=F32)
            s = jnp.where(valid, s * (HEAD_DIM ** -0.5), NEG)
            sink = sink_ref[h]
            m = jnp.maximum(jnp.max(s, axis=-1, keepdims=True), sink)
            p = jnp.exp(s - m)
            denom = jnp.sum(p, axis=-1, keepdims=True) + jnp.exp(sink - m)
            o = jnp.dot(p.astype(BF16), v, preferred_element_type=F32) / denom
            o_ref[:, h * HEAD_DIM:(h + 1) * HEAD_DIM] = o.astype(BF16)


def _swa_mixer(u, sink, nseq, seq):
    nq = seq // SWA_TQ
    return pl.pallas_call(
        functools.partial(_swa_kernel, seq=seq),
        grid=(nseq, nq),
        in_specs=[pl.BlockSpec(memory_space=pltpu.SMEM),
                  pl.BlockSpec((SWA_TQ, GROUP_WIDTH), lambda b, i: (b * nq + i, COL_SWA_Q // GROUP_WIDTH)),
                  pl.BlockSpec((seq, GROUP_WIDTH), lambda b, i: (b, COL_SWA_KV // GROUP_WIDTH))],
        out_specs=pl.BlockSpec((SWA_TQ, GROUP_WIDTH), lambda b, i: (b * nq + i, 0)),
        out_shape=jax.ShapeDtypeStruct((nseq * seq, GROUP_WIDTH), BF16),
        compiler_params=_cparams("parallel", "arbitrary"),
        name="swa_mixer",
    )(sink, u, u)


def _fnet_kernel(x_ref, m_ref, cc_ref, sc_ref, o_ref, z_ref, *, seq):
    r1 = seq // FNET_S2
    half = r1 // 2
    xs = [x_ref[s1 * FNET_S2:(s1 + 1) * FNET_S2, :].astype(F32) for s1 in range(r1)]
    sym = {s1: xs[s1] + xs[r1 - s1] for s1 in range(1, half)}
    asym = {s1: xs[s1] - xs[r1 - s1] for s1 in range(1, half)}

    def weighted(terms):
        acc = None
        for w, t in terms:
            if abs(w) < 1e-9:
                continue
            if abs(w - 1.0) < 1e-9:
                acc = t if acc is None else acc + t
            elif abs(w + 1.0) < 1e-9:
                acc = -t if acc is None else acc - t
            else:
                acc = w * t if acc is None else acc + w * t
        return jnp.zeros((FNET_S2, GROUP_WIDTH), F32) if acc is None else acc

    for k1 in range(half + 1):
        re_terms = [(1.0, xs[0]), (float((-1) ** k1), xs[half])]
        re_terms += [(math.cos(2 * math.pi * k1 * s1 / r1), sym[s1]) for s1 in range(1, half)]
        im_terms = [(-math.sin(2 * math.pi * k1 * s1 / r1), asym[s1]) for s1 in range(1, half)]
        zr = weighted(re_terms).astype(BF16)
        zi = weighted(im_terms)
        z_ref[k1, :FNET_S2, :] = zr
        z_ref[k1, FNET_S2:, :] = zi.astype(BF16)
        if 0 < k1 < half:
            z_ref[r1 - k1, :FNET_S2, :] = zr
            z_ref[r1 - k1, FNET_S2:, :] = (-zi).astype(BF16)

    scale = 1.0 / math.sqrt(seq * FNET_CH)
    cc, sc = cc_ref[...], sc_ref[...]
    for k1 in range(r1):
        y = jnp.dot(m_ref[k1], z_ref[k1], preferred_element_type=F32)
        out = (jnp.dot(y[:FNET_S2].astype(BF16), cc, preferred_element_type=F32)
               + jnp.dot(y[FNET_S2:].astype(BF16), sc, preferred_element_type=F32))
        o_ref[:, k1 * GROUP_WIDTH:(k1 + 1) * GROUP_WIDTH] = (out * scale).astype(BF16)


def _fnet_constants(seq):
    r1 = seq // FNET_S2
    k = (np.arange(r1)[:, None, None] + r1 * np.arange(FNET_S2)[None, :, None]).astype(np.float64)
    s2 = np.arange(FNET_S2)[None, None, :].astype(np.float64)
    ang = 2 * np.pi * ((k * s2) % seq) / seq
    c, s = np.cos(ang), np.sin(ang)
    m = np.concatenate([np.concatenate([c, s], -1), np.concatenate([-s, c], -1)], 1)
    ch = np.arange(GROUP_WIDTH)
    same = (ch[:, None] // FNET_CH) == (ch[None, :] // FNET_CH)
    ang_c = 2 * np.pi * ((ch[:, None] % FNET_CH) * (ch[None, :] % FNET_CH) % FNET_CH) / FNET_CH
    cc = np.where(same, np.cos(ang_c), 0.0)
    sc = np.where(same, np.sin(ang_c), 0.0)
    return (jnp.asarray(m, BF16), jnp.asarray(cc, BF16), jnp.asarray(sc, BF16))


def _fnet_mixer(u, consts, nseq, seq):
    r1 = seq // FNET_S2
    m, cc, sc = consts
    out = pl.pallas_call(
        functools.partial(_fnet_kernel, seq=seq),
        grid=(nseq,),
        in_specs=[pl.BlockSpec((seq, GROUP_WIDTH), lambda b: (b, COL_FNET // GROUP_WIDTH)),
                  pl.BlockSpec((r1, 2 * FNET_S2, 2 * FNET_S2), lambda b: (0, 0, 0)),
                  pl.BlockSpec((GROUP_WIDTH, GROUP_WIDTH), lambda b: (0, 0)),
                  pl.BlockSpec((GROUP_WIDTH, GROUP_WIDTH), lambda b: (0, 0))],
        out_specs=pl.BlockSpec((FNET_S2, r1 * GROUP_WIDTH), lambda b: (b, 0)),
        out_shape=jax.ShapeDtypeStruct((nseq * FNET_S2, r1 * GROUP_WIDTH), BF16),
        scratch_shapes=[pltpu.VMEM((r1, 2 * FNET_S2, GROUP_WIDTH), BF16)],
        compiler_params=_cparams("parallel"),
        name="fnet_mixer",
    )(u, m, cc, sc)
    return out.reshape(nseq * seq, GROUP_WIDTH)


NAT_G = 4


def _nat_geometry(rows, kh):
    wr = NAT_G + kh
    n_groups = rows // NAT_G
    w0 = np.clip(NAT_G * np.arange(n_groups) - kh // 2, 0, rows - wr)
    patterns, pat_id = [], []
    for g in range(n_groups):
        r = NAT_G * g + np.arange(NAT_G)
        kr0 = np.clip(r - kh // 2, 0, rows - kh)
        kr = w0[g] + np.arange(wr)
        valid = (kr[None, :] >= kr0[:, None]) & (kr[None, :] < kr0[:, None] + kh)
        dr = np.where(valid, kr[None, :] - r[:, None] + NAT_KH - 1, 0)
        assert valid.sum(1).min() == kh and dr.min() >= 0 and dr.max() <= 2 * NAT_KH - 2
        for i, (v, d) in enumerate(patterns):
            if np.array_equal(v, valid) and np.array_equal(d, dr):
                pat_id.append(i)
                break
        else:
            pat_id.append(len(patterns))
            patterns.append((valid, dr))
    return w0.astype(np.int32), np.asarray(pat_id, np.int32), patterns


def _nat_bias_kernel(rpb_ref, onehot_ref, o_ref):
    a = rpb_ref[...]
    a1 = a.astype(BF16)
    r1 = a - a1.astype(F32)
    a2 = r1.astype(BF16)
    a3 = (r1 - a2.astype(F32)).astype(BF16)
    e = onehot_ref[...]
    o_ref[...] = (jnp.dot(a1, e, preferred_element_type=F32) + jnp.dot(a2, e, preferred_element_type=F32)
                  + jnp.dot(a3, e, preferred_element_type=F32))


def _nat_bias_table(rpb, patterns):
    n_dr, n_dc = 2 * NAT_KH - 1, 2 * NAT_KW - 1
    c = np.arange(GRID_W)
    dc = np.clip(c[None, :] - c[:, None] + (NAT_KW - 1), 0, n_dc - 1)
    onehot = (np.arange(n_dc)[:, None, None] == dc[None]).astype(np.float32)
    onehot = np.concatenate([onehot.reshape(n_dc, -1), np.zeros((32 - n_dc, GRID_W * GRID_W), np.float32)], 0)
    rows = NAT_HEADS * n_dr
    rows_pad = -(-rows // 8) * 8
    rpb2 = jnp.zeros((rows_pad, 32), F32).at[:rows, :n_dc].set(rpb.reshape(rows, n_dc))
    t = pl.pallas_call(
        _nat_bias_kernel,
        out_shape=jax.ShapeDtypeStruct((rows_pad, GRID_W * GRID_W), F32),
        name="nat_bias",
    )(rpb2, jnp.asarray(onehot, BF16))
    t = t[:rows].reshape(NAT_HEADS, n_dr, GRID_W, GRID_W)
    c0 = np.clip(c - NAT_KW // 2, 0, GRID_W - NAT_KW)
    col_ok = (c[None, :] >= c0[:, None]) & (c[None, :] < c0[:, None] + NAT_KW)
    mask = jnp.asarray(np.where(col_ok, 0.0, NEG), F32)
    tabs = []
    for valid, dr in patterns:
        wr = valid.shape[1]
        win = t[:, dr] + mask[None, None, None]
        win = jnp.where(jnp.asarray(valid)[None, :, :, None, None], win, NEG)
        win = jnp.transpose(win, (0, 1, 3, 2, 4))
        tabs.append(win.reshape(NAT_HEADS, NAT_G * GRID_W, wr * GRID_W))
    return jnp.stack(tabs, 0)


def _nat_kernel(w0_ref, pat_ref, q_ref, k_ref, v_ref, bias_ref, o_ref, *, wk):
    g = pl.program_id(1)
    kbase = pl.multiple_of(w0_ref[g] * GRID_W, GRID_W)
    kk = k_ref[pl.ds(kbase, wk), :]
    vv = v_ref[pl.ds(kbase, wk), :]
    q = q_ref[...]
    tq = q.shape[0]
    lane = lax.broadcasted_iota(jnp.int32, (tq, 128), 1)
    upper, lower = lane >= HEAD_DIM, lane < HEAD_DIM
    for hp in range(NAT_HEADS // 2):
        sl = slice(hp * 128, (hp + 1) * 128)
        q2, k2, v2 = q[:, sl], kk[:, sl], vv[:, sl]
        outs = []
        for r in range(2):
            qm = jnp.where(upper if r else lower, q2, jnp.zeros_like(q2))
            s = lax.dot_general(qm, k2, (((1,), (1,)), ((), ())), preferred_element_type=F32)
            s = s * (HEAD_DIM ** -0.5) + bias_ref[0, 2 * hp + r]
            m = jnp.max(s, axis=-1, keepdims=True)
            p = jnp.exp(s - m)
            denom = jnp.sum(p, axis=-1, keepdims=True)
            outs.append(jnp.dot(p.astype(BF16), v2, preferred_element_type=F32) / denom)
        o_ref[:, sl] = jnp.where(upper, outs[1], outs[0]).astype(BF16)


def _nat_mixer(u, rpb, nseq, seq):
    rows = seq // GRID_W
    kh = min(NAT_KH, rows)
    w0, pat_id, patterns = _nat_geometry(rows, kh)
    bias_tab = _nat_bias_table(rpb, patterns)
    n_groups = rows // NAT_G
    tq = NAT_G * GRID_W
    wk = (NAT_G + kh) * GRID_W
    kv_spec = lambda col: pl.BlockSpec((seq, GROUP_WIDTH), lambda b, g, w0, pat: (b, col // GROUP_WIDTH))
    grid_spec = pltpu.PrefetchScalarGridSpec(
        num_scalar_prefetch=2,
        grid=(nseq, n_groups),
        in_specs=[pl.BlockSpec((tq, GROUP_WIDTH),
                               lambda b, g, w0, pat: (b * n_groups + g, COL_NAT_Q // GROUP_WIDTH)),
                  kv_spec(COL_NAT_K), kv_spec(COL_NAT_V),
                  pl.BlockSpec((1, NAT_HEADS, tq, wk), lambda b, g, w0, pat: (pat[g], 0, 0, 0))],
        out_specs=pl.BlockSpec((tq, GROUP_WIDTH), lambda b, g, w0, pat: (b * n_groups + g, 0)),
    )
    return pl.pallas_call(
        functools.partial(_nat_kernel, wk=wk),
        grid_spec=grid_spec,
        out_shape=jax.ShapeDtypeStruct((nseq * seq, GROUP_WIDTH), BF16),
        compiler_params=_cparams("parallel", "arbitrary"),
        name="nat_mixer",
    )(jnp.asarray(w0), jnp.asarray(pat_id), u, u, u, bias_tab)


ROUTE_W = 128


def _out_proj_kernel(ya_ref, yb_ref, yc_ref, yd_ref, x_ref, mg_ref, wo_ref, g2_ref, wr_ref, br_ref, tri_ref,
                     x2_ref, h2_ref, route_ref, count_ref, base_ref):
    @pl.when(pl.program_id(0) == 0)
    def _():
        base_ref[...] = jnp.zeros_like(base_ref)

    acc = x_ref[...]
    for j, y_ref in enumerate((ya_ref, yb_ref, yc_ref, yd_ref)):
        y = y_ref[...].astype(F32)
        y = y * lax.rsqrt(jnp.mean(y * y, axis=-1, keepdims=True) + EPS)
        y = y * mg_ref[:, j * GROUP_WIDTH:(j + 1) * GROUP_WIDTH]
        acc = acc + jnp.dot(y.astype(BF16), wo_ref[j * GROUP_WIDTH:(j + 1) * GROUP_WIDTH, :],
                            preferred_element_type=F32)
    x2_ref[...] = acc
    h2 = acc * lax.rsqrt(jnp.mean(acc * acc, axis=-1, keepdims=True) + EPS) * g2_ref[...]
    h2_ref[...] = h2

    a1 = h2.astype(BF16)
    a2 = (h2 - a1.astype(F32)).astype(BF16)
    w1, w2 = wr_ref[0], wr_ref[1]
    dot = lambda a, w: jnp.dot(a, w, preferred_element_type=F32)
    logits = dot(a1, w1) + (dot(a1, w2) + dot(a2, w1)) + br_ref[...]

    lane = lax.broadcasted_iota(jnp.int32, logits.shape, 1).astype(F32)
    big = jnp.float32(ROUTE_W)
    is_g = lane < N_EXPERT_GROUPS
    gl = jnp.where(is_g, logits, NEG)
    gm = jnp.max(gl, axis=-1, keepdims=True)
    grp = jnp.min(jnp.where(gl == gm, lane, big), axis=-1, keepdims=True)
    gsum = jnp.sum(jnp.where(is_g, jnp.exp(gl - gm), 0.0), axis=-1, keepdims=True)
    p_grp = 1.0 / gsum
    lo = N_EXPERT_GROUPS + EXPERTS_PER_GROUP * grp
    emask = (lane >= lo) & (lane < lo + EXPERTS_PER_GROUP)
    el = jnp.where(emask, logits, NEG)
    l1 = jnp.max(el, axis=-1, keepdims=True)
    i1 = jnp.min(jnp.where(emask & (el == l1), lane, big), axis=-1, keepdims=True)
    rest = emask & (lane != i1)
    el2 = jnp.where(rest, logits, NEG)
    l2 = jnp.max(el2, axis=-1, keepdims=True)
    i2 = jnp.min(jnp.where(rest & (el2 == l2), lane, big), axis=-1, keepdims=True)
    e21 = jnp.exp(l2 - l1)
    gate0 = p_grp / (1.0 + e21)
    gate1 = p_grp * e21 / (1.0 + e21)
    e0 = i1 - N_EXPERT_GROUPS
    e1 = i2 - N_EXPERT_GROUPS

    hit0 = lane == e0
    hit1 = lane == e1
    onehot = jnp.where(hit0 | hit1, 1.0, 0.0)
    before = jnp.dot(tri_ref[...], onehot.astype(BF16), preferred_element_type=F32) + base_ref[0:1, :]
    rank0 = jnp.sum(jnp.where(hit0, before, 0.0), axis=-1, keepdims=True)
    rank1 = jnp.sum(jnp.where(hit1, before, 0.0), axis=-1, keepdims=True)
    new_base = base_ref[0:1, :] + jnp.sum(onehot, axis=0, keepdims=True)
    base_ref[...] = jnp.broadcast_to(new_base, base_ref.shape)
    count_ref[...] = jnp.broadcast_to(new_base, count_ref.shape)

    rec = jnp.zeros_like(logits)
    for col, val in enumerate((e0, e1, gate0, gate1, rank0, rank1)):
        rec = jnp.where(lane == col, val, rec)
    route_ref[...] = rec


def _out_proj(ys, x, mix_g, wo_bf16, g2, wr3, br, tri):
    n = x.shape[0]
    tile = lambda w: pl.BlockSpec((TM, w), lambda i: (i, 0))
    const = lambda *shape: pl.BlockSpec(shape, lambda i: (0,) * len(shape))
    return pl.pallas_call(
        _out_proj_kernel,
        grid=(n // TM,),
        in_specs=[tile(GROUP_WIDTH)] * 4 + [tile(D_MODEL), const(1, D_MODEL), const(D_MODEL, D_MODEL),
                                           const(1, D_MODEL), const(2, D_MODEL, ROUTE_W), const(1, ROUTE_W),
                                           const(TM, TM)],
        out_specs=[tile(D_MODEL), tile(D_MODEL), tile(ROUTE_W), const(8, ROUTE_W)],
        out_shape=[jax.ShapeDtypeStruct((n, D_MODEL), F32), jax.ShapeDtypeStruct((n, D_MODEL), F32),
                   jax.ShapeDtypeStruct((n, ROUTE_W), F32), jax.ShapeDtypeStruct((8, ROUTE_W), F32)],
        scratch_shapes=[pltpu.VMEM((8, ROUTE_W), F32)],
        compiler_params=_cparams("arbitrary"),
        name="out_proj_router",
    )(*ys, x, mix_g.reshape(1, D_MODEL), wo_bf16, g2.reshape(1, D_MODEL), wr3, br, tri)


def _router_weights(wg, bg, we, be):
    w = jnp.concatenate([wg, we, jnp.zeros((D_MODEL, ROUTE_W - N_EXPERT_GROUPS - N_EXPERTS), F32)], -1)
    w1 = w.astype(BF16)
    w2 = (w - w1.astype(F32)).astype(BF16)
    b = jnp.concatenate([bg, be, jnp.zeros((ROUTE_W - N_EXPERT_GROUPS - N_EXPERTS,), F32)]).reshape(1, ROUTE_W)
    return jnp.stack([w1, w2], 0), b


def _moe_kernel(layer_ref, be_ref, valid_ref, x_ref, wg_ref, wu_ref, wd_ref, o_ref, wg_bf, wu_bf, wd_bf):
    b = pl.program_id(0)

    @pl.when(jnp.logical_or(b == 0, be_ref[b] != be_ref[jnp.maximum(b - 1, 0)]))
    def _():
        wg_bf[...] = wg_ref[0, 0].astype(BF16)
        wu_bf[...] = wu_ref[0, 0].astype(BF16)
        wd_bf[...] = wd_ref[0, 0].astype(BF16)

    row = lax.broadcasted_iota(jnp.int32, (MOE_BLOCK, 1), 0)
    x = jnp.where(row < valid_ref[b], x_ref[...], 0.0).astype(BF16)
    gate = jnp.dot(x, wg_bf[...], preferred_element_type=F32)
    up = jnp.dot(x, wu_bf[...], preferred_element_type=F32)
    hid = (gate / (1.0 + jnp.exp(-gate))) * up
    o_ref[...] = jnp.dot(hid.astype(BF16), wd_bf[...], preferred_element_type=F32)


def _moe_blocks(layer, block_expert, block_valid, x_sorted, wg, wu, wd):
    nb = block_expert.shape[0]
    wspec = lambda shape: pl.BlockSpec((1, 1) + shape, lambda b, layer, be, valid: (layer[0], be[b], 0, 0))
    grid_spec = pltpu.PrefetchScalarGridSpec(
        num_scalar_prefetch=3,
        grid=(nb,),
        in_specs=[pl.BlockSpec((MOE_BLOCK, D_MODEL), lambda b, layer, be, valid: (b, 0)),
                  wspec((D_MODEL, D_EXPERT)), wspec((D_MODEL, D_EXPERT)), wspec((D_EXPERT, D_MODEL))],
        out_specs=pl.BlockSpec((MOE_BLOCK, D_MODEL), lambda b, layer, be, valid: (b, 0)),
        scratch_shapes=[pltpu.VMEM((D_MODEL, D_EXPERT), BF16), pltpu.VMEM((D_MODEL, D_EXPERT), BF16),
                        pltpu.VMEM((D_EXPERT, D_MODEL), BF16)],
    )
    return pl.pallas_call(
        _moe_kernel,
        grid_spec=grid_spec,
        out_shape=jax.ShapeDtypeStruct((nb * MOE_BLOCK, D_MODEL), F32),
        compiler_params=_cparams("arbitrary"),
        name="moe_blocks",
    )(layer, block_expert, block_valid, x_sorted, wg, wu, wd)


def _final_combine_kernel(x_ref, route_ref, g_ref, y0_ref, y1_ref, o_ref):
    route = route_ref[...]
    x = x_ref[...] + route[:, 2:3] * y0_ref[...] + route[:, 3:4] * y1_ref[...]
    o_ref[...] = x * lax.rsqrt(jnp.mean(x * x, axis=-1, keepdims=True) + EPS) * g_ref[...]


def _final_combine(x2, route, y_rows, final_g, first_tile, tiles):
    n_tiles = x2.shape[0] // TM
    tile = lambda w, shift: pl.BlockSpec((TM, w), lambda i: (i + shift, 0))
    return pl.pallas_call(
        _final_combine_kernel,
        grid=(tiles,),
        in_specs=[tile(D_MODEL, first_tile), tile(ROUTE_W, first_tile),
                  pl.BlockSpec((1, D_MODEL), lambda i: (0, 0)),
                  tile(D_MODEL, first_tile), tile(D_MODEL, first_tile + n_tiles)],
        out_specs=tile(D_MODEL, 0),
        out_shape=jax.ShapeDtypeStruct((tiles * TM, D_MODEL), F32),
        compiler_params=_cparams("parallel"),
        name="final_combine",
    )(x2, route, final_g.reshape(1, D_MODEL), y_rows, y_rows)


def _routing_tables(route, counts, n):
    e = route[:, 0:2].astype(jnp.int32)
    rank = route[:, 4:6].astype(jnp.int32)
    cnt = counts[0, :N_EXPERTS].astype(jnp.int32)
    padded = (cnt + MOE_BLOCK - 1) // MOE_BLOCK * MOE_BLOCK
    pend = jnp.cumsum(padded)
    pstart = pend - padded
    dest = pstart[e] + rank
    n_blocks = -(-(2 * n) // MOE_BLOCK) + N_EXPERTS
    block_first = jnp.arange(n_blocks, dtype=jnp.int32) * MOE_BLOCK
    block_expert = jnp.minimum(jnp.sum((pend[None, :] <= block_first[:, None]).astype(jnp.int32), axis=1),
                               N_EXPERTS - 1)
    block_valid = jnp.clip((pstart + cnt)[block_expert] - block_first, 0, MOE_BLOCK)
    dest_flat = jnp.transpose(dest, (1, 0)).reshape(2 * n)
    return block_expert, block_valid, dest_flat


_EXPERT_WEIGHTS = ('expert_w_gate', 'expert_w_up', 'expert_w_down')


def _trunk(x, p, nseq, seq):
    n = nseq * seq
    depth = p['w_in'].shape[0]
    rope = _rope_tables(seq)
    fnet_consts = _fnet_constants(seq)
    tri = jnp.asarray(np.tril(np.ones((TM, TM), np.float32), -1), BF16)
    small = {k: v for k, v in p.items() if k != 'final_norm_g' and k not in _EXPERT_WEIGHTS}

    def mix_and_experts(x, u, w, l):
        ya = _conv_mixer(u, w['conv_w'], w['conv_b'], w['conv_norm_g'], w['conv_norm_b'], nseq, seq)
        yb = _swa_mixer(u, w['attn_sink'], nseq, seq)
        yc = _fnet_mixer(u, fnet_consts, nseq, seq)
        yd = _nat_mixer(u, w['nat_rpb'], nseq, seq)
        wr, br = _router_weights(w['router_group_w'], w['router_group_b'],
                                 w['router_expert_w'], w['router_expert_b'])
        x2, h2, route, counts = _out_proj((ya, yb, yc, yd), x, w['mix_norm_g'], w['w_out'].astype(BF16),
                                          w['norm2_g'], wr, br, tri)
        block_expert, block_valid, dest_flat = _routing_tables(route, counts, n)
        x_sorted = _sc_scatter_pairs(h2, dest_flat, block_expert.shape[0] * MOE_BLOCK)
        out_sorted = _moe_blocks(jnp.reshape(l, (1,)).astype(jnp.int32), block_expert, block_valid, x_sorted,
                                 p['expert_w_gate'], p['expert_w_up'], p['expert_w_down'])
        return x2, route, _sc_gather(out_sorted, dest_flat)

    w0 = {k: v[0] for k, v in small.items()}
    u = _in_proj(x, w0['norm1_g'], w0['w_in'].astype(BF16), rope, seq)
    carry = mix_and_experts(x, u, w0, jnp.int32(0))

    def body(carry, l):
        w = {k: lax.dynamic_index_in_dim(v, l, 0, keepdims=False) for k, v in small.items()}
        x, u = _in_proj_combine(*carry, w['norm1_g'], w['w_in'].astype(BF16), rope, seq)
        return mix_and_experts(x, u, w, l), None

    if depth > 1:
        carry, _ = lax.scan(body, carry, jnp.arange(1, depth, dtype=jnp.int32))
    return carry


@jax.jit
def _forward(x_prompt, x_sample, p):
    bp, seq, d = x_prompt.shape
    bs = x_sample.shape[0]
    assert x_sample.shape[1] == seq and d == D_MODEL and seq % TM == 0
    x = jnp.concatenate([x_prompt.reshape(bp * seq, d), x_sample.reshape(bs * seq, d)], 0)
    x2, route, y_rows = _trunk(x, p, bp + bs, seq)
    tiles_p, tiles_s = bp * seq // TM, bs * seq // TM
    y_prompt = _final_combine(x2, route, y_rows, p['final_norm_g'], 0, tiles_p)
    y_sample = _final_combine(x2, route, y_rows, p['final_norm_g'], tiles_p, tiles_s)
    return (y_prompt.reshape(bp, seq, d), y_sample.reshape(bs, seq, d))


def kernel(x_prompt, x_sample, norm1_g, w_in, conv_w, conv_b, conv_norm_g, conv_norm_b, attn_sink, nat_rpb,
           mix_norm_g, w_out, norm2_g, router_group_w, router_group_b, router_expert_w, router_expert_b,
           expert_w_gate, expert_w_up, expert_w_down, final_norm_g):
    p = dict(norm1_g=norm1_g, w_in=w_in, conv_w=conv_w, conv_b=conv_b, conv_norm_g=conv_norm_g,
             conv_norm_b=conv_norm_b, attn_sink=attn_sink, nat_rpb=nat_rpb, mix_norm_g=mix_norm_g, w_out=w_out,
             norm2_g=norm2_g, router_group_w=router_group_w, router_group_b=router_group_b,
             router_expert_w=router_expert_w, router_expert_b=router_expert_b, expert_w_gate=expert_w_gate,
             expert_w_up=expert_w_up, expert_w_down=expert_w_down, final_norm_g=final_norm_g)
    return _forward(x_prompt, x_sample, p)
```
